```python
import math
import jax, jax.numpy as jnp
from jax import lax
import numpy as np

D_MODEL = 1024
BATCH = 8
SEQ = 2048
DEPTH = 1

NSA_HEADS = 16
NSA_KV_GROUPS = 4
NSA_Q_PER_GROUP = NSA_HEADS // NSA_KV_GROUPS
NSA_HEAD_DIM = 64
CMP_BLOCK = 32
CMP_STRIDE = 16
CMP_HIDDEN = 4 * NSA_HEAD_DIM
SEL_BLOCK = 64
SEL_TOPK = 8
WINDOW = 512
Q_BLOCK = 64
SSM_D_INNER = 2 * D_MODEL
SSM_HEAD_DIM = 64
SSM_HEADS = SSM_D_INNER // SSM_HEAD_DIM
SSM_GROUPS = 4
SSM_STATE = 128
SSM_CONV = 4
SSM_CHUNK = 128
FFN_HIDDEN = 2816
FFN_CONV = 3
Q_W = NSA_HEADS * NSA_HEAD_DIM
KV_W = NSA_KV_GROUPS * NSA_HEAD_DIM
NSA_GATE_W = 3 * NSA_HEADS
SSM_BC_W = SSM_GROUPS * SSM_STATE
SSM_XBC_W = SSM_D_INNER + 2 * SSM_BC_W
IN_SIZES = (Q_W, KV_W, KV_W, KV_W, KV_W, KV_W, KV_W, NSA_GATE_W, SSM_D_INNER, SSM_XBC_W, SSM_HEADS, D_MODEL, D_MODEL)
IN_W = Q_W + 6 * KV_W + NSA_GATE_W + SSM_D_INNER + SSM_XBC_W + SSM_HEADS + 2 * D_MODEL
EPS = 1e-6
NEG_INF = -1e30

kernel_name = 'hybrid_nsa_ssd_convglu_block'


def rms_norm(x, w):
    xf = x.astype(jnp.float32)
    y = xf * lax.rsqrt(jnp.mean(xf * xf, axis=-1, keepdims=True) + EPS)
    return (y * w.astype(jnp.float32)).astype(x.dtype)


def causal_depthwise_conv(x, w, b):
    width = w.shape[0]
    y = lax.conv_general_dilated(x, w[:, None, :].astype(x.dtype), window_strides=(1,),
                                 padding=[(width - 1, 0)], dimension_numbers=('NWC', 'WIO', 'NWC'),
                                 feature_group_count=x.shape[-1])
    return y + b.astype(x.dtype)


def split_columns(u, sizes):
    return jnp.split(u, np.cumsum(sizes)[:-1].tolist(), axis=-1)


def alibi_slopes():
    h = jnp.arange(1, NSA_HEADS + 1, dtype=jnp.float32)
    return jnp.exp2(-8.0 * h / NSA_HEADS).reshape(NSA_KV_GROUPS, NSA_Q_PER_GROUP)


def masked_softmax(s, mask):
    return jax.nn.softmax(jnp.where(mask, s, NEG_INF), axis=-1)


def compress_blocks(kv, tok_idx, pe, w1, w2):
    blk = kv[:, tok_idx] + pe[None, None, :, None, :].astype(kv.dtype)
    b, nc = blk.shape[:2]
    flat = jnp.transpose(blk, (0, 1, 3, 2, 4)).reshape(b, nc, NSA_KV_GROUPS, CMP_BLOCK * NSA_HEAD_DIM)
    return jax.nn.silu(flat @ w1) @ w2


def nsa_mixer(q, k_cmp, v_cmp, k_sel, v_sel, k_win, v_win, g_branch,
              q_gain, k_gain, pe_k, pe_v, wk1, wk2, wv1, wv2):
    b, s = q.shape[:2]
    G, R, dh = NSA_KV_GROUPS, NSA_Q_PER_GROUP, NSA_HEAD_DIM
    f32 = jnp.float32
    q = rms_norm(q.reshape(b, s, G, R, dh), q_gain) * (dh ** -0.5)
    q = jnp.transpose(q, (0, 2, 3, 1, 4))

    def kv_heads(u):
        return u.reshape(b, s, G, dh)

    nc = (s - CMP_BLOCK) // CMP_STRIDE + 1
    cmp_start = jnp.arange(nc) * CMP_STRIDE
    tok_idx = cmp_start[:, None] + jnp.arange(CMP_BLOCK)[None, :]
    cmp_end = cmp_start + CMP_BLOCK - 1
    kc = rms_norm(compress_blocks(kv_heads(k_cmp), tok_idx, pe_k, wk1, wk2), k_gain[0]).transpose(0, 2, 1, 3)
    vc = compress_blocks(kv_heads(v_cmp), tok_idx, pe_v, wv1, wv2).transpose(0, 2, 1, 3)

    nb = s // SEL_BLOCK
    n_sel = min(SEL_TOPK, nb)
    ks = rms_norm(kv_heads(k_sel), k_gain[1]).transpose(0, 2, 1, 3).reshape(b, G, nb, SEL_BLOCK, dh)
    vs = kv_heads(v_sel).transpose(0, 2, 1, 3).reshape(b, G, nb, SEL_BLOCK, dh)
    blk_start = jnp.arange(nb) * SEL_BLOCK
    overlap = ((cmp_start[:, None] <= blk_start[None, :] + SEL_BLOCK - 1)
               & (cmp_end[:, None] >= blk_start[None, :])).astype(f32)

    pad = ((0, 0), (0, 0), (WINDOW, 0), (0, 0))
    kw = jnp.pad(rms_norm(kv_heads(k_win), k_gain[2]).transpose(0, 2, 1, 3), pad)
    vw = jnp.pad(kv_heads(v_win).transpose(0, 2, 1, 3), pad)

    slopes = alibi_slopes()[None, :, :, None, None]
    nqb = s // Q_BLOCK
    q_blocks = jnp.moveaxis(q.reshape(b, G, R, nqb, Q_BLOCK, dh), 3, 0)
    gates = jax.nn.sigmoid(g_branch.astype(f32)).reshape(b, nqb, Q_BLOCK, G, R, 3)
    g_blocks = jnp.transpose(gates, (1, 0, 3, 4, 2, 5))
    bi = jnp.arange(b)[:, None, None, None]
    gi = jnp.arange(G)[None, :, None, None]

    def attend_block(args):
        qb, gb, i = args
        start = i * Q_BLOCK
        t = start + jnp.arange(Q_BLOCK)
        s_c = jnp.einsum('bgrqd,bgcd->bgrqc', qb, kc).astype(f32)
        d_c = t[:, None] - cmp_end[None, :]
        vis_c = d_c >= 0
        p_c = masked_softmax(s_c - slopes * d_c, vis_c) * jnp.any(vis_c, axis=-1, keepdims=True)
        o_c = jnp.einsum('bgrqc,bgcd->bgrqd', p_c.astype(vc.dtype), vc)
        imp = jnp.einsum('bgrqc,cj->bgqj', p_c, overlap)
        cur = t // SEL_BLOCK
        j = jnp.arange(nb)[None, :]
        forced = (j == 0) | (j == cur[:, None]) | (j == cur[:, None] - 1)
        causal_blk = j * SEL_BLOCK <= t[:, None]
        score = jnp.where(forced, jnp.inf, jnp.where(causal_blk, imp, -jnp.inf))
        _, idx = lax.top_k(score, n_sel)
        k_g = ks[bi, gi, idx].reshape(b, G, Q_BLOCK, n_sel * SEL_BLOCK, dh)
        v_g = vs[bi, gi, idx].reshape(b, G, Q_BLOCK, n_sel * SEL_BLOCK, dh)
        pos = (idx[..., None] * SEL_BLOCK + jnp.arange(SEL_BLOCK)).reshape(b, G, Q_BLOCK, n_sel * SEL_BLOCK)
        d_s = (t[:, None] - pos)[:, :, None]
        s_s = jnp.einsum('bgrqd,bgqkd->bgrqk', qb, k_g).astype(f32)
        p_s = masked_softmax(s_s - slopes * d_s, d_s >= 0)
        o_s = jnp.einsum('bgrqk,bgqkd->bgrqd', p_s.astype(v_g.dtype), v_g)
        k_w = lax.dynamic_slice_in_dim(kw, start, Q_BLOCK + WINDOW, axis=2)
        v_w = lax.dynamic_slice_in_dim(vw, start, Q_BLOCK + WINDOW, axis=2)
        pos_w = start - WINDOW + jnp.arange(Q_BLOCK + WINDOW)
        d_w = t[:, None] - pos_w[None, :]
        vis_w = (d_w >= 0) & (d_w < WINDOW) & (pos_w[None, :] >= 0)
        s_w = jnp.einsum('bgrqd,bgkd->bgrqk', qb, k_w).astype(f32)
        p_w = masked_softmax(s_w - slopes * d_w, vis_w)
        o_w = jnp.einsum('bgrqk,bgkd->bgrqd', p_w.astype(v_w.dtype), v_w)
        return gb[..., 0:1] * o_c + gb[..., 1:2] * o_s + gb[..., 2:3] * o_w

    o = lax.map(attend_block, (q_blocks, g_blocks, jnp.arange(nqb)))
    return jnp.transpose(o, (1, 0, 4, 2, 3, 5)).reshape(b, s, NSA_HEADS * dh).astype(v_cmp.dtype)


def segsum(a):
    n = a.shape[-1]
    a_rep = jnp.broadcast_to(a[..., :, None], a.shape + (n,))
    a_rep = jnp.where(jnp.tril(jnp.ones((n, n), bool), -1), a_rep, 0.0)
    seg = jnp.cumsum(a_rep, axis=-2)
    return jnp.where(jnp.tril(jnp.ones((n, n), bool), 0), seg, -jnp.inf)


def ssd_chunked(x, dt, a, bm, cm):
    b, s = x.shape[:2]
    nc, L, G, E = s // SSM_CHUNK, SSM_CHUNK, SSM_GROUPS, SSM_HEADS // SSM_GROUPS
    f32 = jnp.float32
    x = x.reshape(b, nc, L, G, E, SSM_HEAD_DIM).astype(f32)
    dt = dt.reshape(b, nc, L, G, E)
    xdt = x * dt[..., None]
    a_dt = jnp.transpose(dt * a.reshape(G, E), (0, 3, 4, 1, 2))
    bm = bm.reshape(b, nc, L, G, SSM_STATE).astype(f32)
    cm = cm.reshape(b, nc, L, G, SSM_STATE).astype(f32)
    a_cum = jnp.cumsum(a_dt, axis=-1)
    decay_mat = jnp.exp(segsum(a_dt))
    cb = jnp.einsum('bclgn,bcsgn->bgcls', cm, bm)
    y_diag = jnp.einsum('bgecls,bcsgep->bclgep', cb[:, :, None] * decay_mat, xdt)
    decay_states = jnp.exp(a_cum[..., -1:] - a_cum)
    states = jnp.einsum('bclgn,bgecl,bclgep->bcgepn', bm, decay_states, xdt)
    chunk_decay = jnp.exp(a_cum[..., -1])

    def step(h, inp):
        st, dec = inp
        return dec[..., None, None] * h + st, h

    h0 = jnp.zeros((b, G, E, SSM_HEAD_DIM, SSM_STATE), f32)
    _, prev = lax.scan(step, h0, (jnp.moveaxis(states, 1, 0), jnp.moveaxis(chunk_decay, 3, 0)))
    prev = jnp.moveaxis(prev, 0, 1)
    y_off = jnp.einsum('bclgn,bcgepn,bgecl->bclgep', cm, prev, jnp.exp(a_cum))
    return (y_diag + y_off).reshape(b, s, SSM_HEADS, SSM_HEAD_DIM)


def ssd_mixer(z, xbc, dt_raw, conv_w, conv_b, dt_bias, a_log, d_skip, norm_w):
    b, s = z.shape[:2]
    f32 = jnp.float32
    xbc = jax.nn.silu(causal_depthwise_conv(xbc, conv_w, conv_b))
    xs, bm, cm = split_columns(xbc, (SSM_D_INNER, SSM_BC_W, SSM_BC_W))
    xs = xs.reshape(b, s, SSM_HEADS, SSM_HEAD_DIM)
    bm = bm.reshape(b, s, SSM_GROUPS, SSM_STATE)
    cm = cm.reshape(b, s, SSM_GROUPS, SSM_STATE)
    dt = jax.nn.softplus(dt_raw.astype(f32) + dt_bias.astype(f32))
    a = -jnp.exp(a_log.astype(f32))
    y = ssd_chunked(xs, dt, a, bm, cm) + d_skip.astype(f32)[:, None] * xs.astype(f32)
    y = y.reshape(b, s, SSM_D_INNER) * jax.nn.silu(z.astype(f32))
    yg = y.reshape(b, s, SSM_GROUPS, SSM_D_INNER // SSM_GROUPS)
    yg = yg * lax.rsqrt(jnp.mean(yg * yg, axis=-1, keepdims=True) + EPS)
    return (yg.reshape(b, s, SSM_D_INNER) * norm_w.astype(f32)).astype(z.dtype)


def conv_glu_ffn(h, w_up, conv_w, conv_b, w_down):
    u = causal_depthwise_conv(h @ w_up, conv_w, conv_b)
    gate, val = jnp.split(u, 2, axis=-1)
    return (jax.nn.silu(gate) * val) @ w_down


def setup_inputs(seed: int = 0) -> dict:
    key = jax.random.key(seed)
    k = jax.random.split(key, 32)
    f32 = jnp.float32
    L = DEPTH
    dh = NSA_HEAD_DIM

    def nrm(kk, shape, scale):
        return jax.random.normal(kk, shape, f32) * scale

    dt0 = jnp.exp(jax.random.uniform(k[14], (L, SSM_HEADS), f32, math.log(1e-3), math.log(1e-1)))
    return {
        'x': nrm(k[0], (BATCH, SEQ, D_MODEL), 1.0),
        'norm1_w': 1.0 + nrm(k[1], (L, D_MODEL), 0.02),
        'w_in': nrm(k[2], (L, D_MODEL, IN_W), D_MODEL ** -0.5),
        'nsa_q_gain': 1.0 + nrm(k[3], (L, dh), 0.02),
        'nsa_k_gain': 1.0 + nrm(k[4], (L, 3, dh), 0.02),
        'cmp_pe_k': nrm(k[5], (L, CMP_BLOCK, dh), 0.02),
        'cmp_pe_v': nrm(k[6], (L, CMP_BLOCK, dh), 0.02),
        'cmp_wk1': nrm(k[7], (L, CMP_BLOCK * dh, CMP_HIDDEN), (CMP_BLOCK * dh) ** -0.5),
        'cmp_wk2': nrm(k[8], (L, CMP_HIDDEN, dh), CMP_HIDDEN ** -0.5),
        'cmp_wv1': nrm(k[9], (L, CMP_BLOCK * dh, CMP_HIDDEN), (CMP_BLOCK * dh) ** -0.5),
        'cmp_wv2': nrm(k[10], (L, CMP_HIDDEN, dh), CMP_HIDDEN ** -0.5),
        'ssm_conv_w': nrm(k[11], (L, SSM_CONV, SSM_XBC_W), SSM_CONV ** -0.5),
        'ssm_conv_b': nrm(k[12], (L, SSM_XBC_W), 0.01),
        'ssm_dt_bias': dt0 + jnp.log(-jnp.expm1(-dt0)),
        'ssm_a_log': jnp.log(jax.random.uniform(k[13], (L, SSM_HEADS), f32, 1.0, 16.0)),
        'ssm_d': 1.0 + nrm(k[15], (L, SSM_HEADS), 0.01),
        'ssm_norm_w': 1.0 + nrm(k[16], (L, SSM_D_INNER), 0.02),
        'w_proj_a': nrm(k[17], (L, Q_W, D_MODEL), Q_W ** -0.5),
        'w_proj_b': nrm(k[18], (L, SSM_D_INNER, D_MODEL), SSM_D_INNER ** -0.5),
        'w_out': nrm(k[19], (L, D_MODEL, D_MODEL), D_MODEL ** -0.5),
        'norm2_w': 1.0 + nrm(k[20], (L, D_MODEL), 0.02),
        'ffn_w_up': nrm(k[21], (L, D_MODEL, 2 * FFN_HIDDEN), D_MODEL ** -0.5),
        'ffn_conv_w': nrm(k[22], (L, FFN_CONV, 2 * FFN_HIDDEN), FFN_CONV ** -0.5),
        'ffn_conv_b': nrm(k[23], (L, 2 * FFN_HIDDEN), 0.01),
        'ffn_w_down': nrm(k[24], (L, FFN_HIDDEN, D_MODEL), FFN_HIDDEN ** -0.5),
    }


def reference(x, norm1_w, w_in, nsa_q_gain, nsa_k_gain, cmp_pe_k, cmp_pe_v, cmp_wk1, cmp_wk2,
              cmp_wv1, cmp_wv2, ssm_conv_w, ssm_conv_b, ssm_dt_bias, ssm_a_log, ssm_d, ssm_norm_w,
              w_proj_a, w_proj_b, w_out, norm2_w, ffn_w_up, ffn_conv_w, ffn_conv_b, ffn_w_down):
    for layer in range(DEPTH):
        h = rms_norm(x, norm1_w[layer])
        (q, k_c, v_c, k_s, v_s, k_w, v_w, g_nsa, z, xbc, dt_raw,
         gate_a, gate_b) = split_columns(h @ w_in[layer], IN_SIZES)
        y_a = nsa_mixer(q, k_c, v_c, k_s, v_s, k_w, v_w, g_nsa, nsa_q_gain[layer], nsa_k_gain[layer],
                        cmp_pe_k[layer], cmp_pe_v[layer], cmp_wk1[layer], cmp_wk2[layer],
                        cmp_wv1[layer], cmp_wv2[layer])
        y_b = ssd_mixer(z, xbc, dt_raw, ssm_conv_w[layer], ssm_conv_b[layer], ssm_dt_bias[layer],
                        ssm_a_log[layer], ssm_d[layer], ssm_norm_w[layer])
        merged = jax.nn.sigmoid(gate_a) * (y_a @ w_proj_a[layer]) + jax.nn.sigmoid(gate_b) * (y_b @ w_proj_b[layer])
        x = x + merged @ w_out[layer]
        x = x + conv_glu_ffn(rms_norm(x, norm2_w[layer]), ffn_w_up[layer], ffn_conv_w[layer],
                             ffn_conv_b[layer], ffn_w_down[layer])
    return x
```

```python
import functools

import jax
import jax.numpy as jnp
from jax import lax
from jax.experimental import pallas as pl
from jax.experimental.pallas import tpu as pltpu

F32 = jnp.float32
BF16 = jnp.bfloat16
I32 = jnp.int32

D_MODEL = 1024
NSA_HEADS = 16
NSA_G = 4
NSA_R = 4
DH = 64
CMP_BLOCK = 32
CMP_STRIDE = 16
CMP_HIDDEN = 256
SEL_BLOCK = 64
SEL_TOPK = 8
WINDOW = 512
SSM_D_INNER = 2048
SSM_P = 64
SSM_HEADS = 32
SSM_G = 4
SSM_E = SSM_HEADS // SSM_G
SSM_N = 128
SSM_CONV = 4
SSM_L = 128
FFN_HIDDEN = 2816
FFN_CONV = 3
EPS = 1e-6
NEG_INF = -1e30

COL_Z = 0
COL_XS = 2048
COL_Q = 4096
COL_GA = 5120
COL_GB = 6144
COL_BM = 7168
COL_CM = 7680
COL_KV = 8192
COL_MISC = 9728
IN_W_PAD = 9856
MISC_W = 128
DT_OFF = 3 * NSA_HEADS

VMEM_LIMIT = 56 * 1024 * 1024


def _cparams(sem):
    return pltpu.CompilerParams(dimension_semantics=sem, vmem_limit_bytes=VMEM_LIMIT)


def _sigmoid(x):
    return 1.0 / (1.0 + jnp.exp(-x))


def _silu(x):
    return x * _sigmoid(x)


def _split3(a):
    hi = a.astype(BF16)
    r1 = a - hi.astype(F32)
    mid = r1.astype(BF16)
    lo = (r1 - mid.astype(F32)).astype(BF16)
    return hi, mid, lo


def _dot(a, b):
    return jnp.dot(a, b, preferred_element_type=F32)


def _dot_nt(a, b):
    return lax.dot_general(a, b, (((1,), (1,)), ((), ())), preferred_element_type=F32)


def _dot_exact_r(a, m):
    hi, mid, lo = _split3(a)
    return _dot(hi, m) + _dot(mid, m) + _dot(lo, m)


def _dot_exact_l(m, a):
    hi, mid, lo = _split3(a)
    return _dot(m, hi) + _dot(m, mid) + _dot(m, lo)


def _inproj_kernel(x_ref, nw_ref, w_ref, o_ref, h_ref):
    @pl.when(pl.program_id(1) == 0)
    def _():
        x = x_ref[...]
        ms = jnp.mean(x * x, axis=-1, keepdims=True)
        h_ref[...] = ((x * lax.rsqrt(ms + EPS)) * nw_ref[...]).astype(BF16)

    o_ref[...] = _dot(h_ref[...], w_ref[...])


def _in_proj(x2, nw, w, tm, tn):
    t, d = x2.shape
    n = w.shape[1]
    return pl.pallas_call(
        _inproj_kernel,
        grid=(t // tm, n // tn),
        in_specs=[
            pl.BlockSpec((tm, d), lambda i, j: (i, 0)),
            pl.BlockSpec((1, d), lambda i, j: (0, 0)),
            pl.BlockSpec((d, tn), lambda i, j: (0, j)),
        ],
        out_specs=pl.BlockSpec((tm, tn), lambda i, j: (i, j)),
        out_shape=jax.ShapeDtypeStruct((t, n), F32),
        scratch_shapes=[pltpu.VMEM((tm, d), BF16)],
        compiler_params=_cparams(("arbitrary", "arbitrary")),
        name="in_proj",
    )(x2, nw, w)


def _prep_kernel(sel_ref, win_ref, misc_ref, kg_ref, ks_ref, vs_ref, kw_ref, vw_ref, gs_ref):
    def knorm(k, gain):
        ms = jnp.mean(k * k, axis=-1, keepdims=True)
        return ((k * lax.rsqrt(ms + EPS)) * gain).astype(BF16)

    sel = sel_ref[...]
    win = win_ref[...]
    sig = _sigmoid(misc_ref[...])
    for g in range(NSA_G):
        lo = g * DH
        ks_ref[0, g] = knorm(sel[:, lo:lo + DH], kg_ref[1:2, :])
        vs_ref[0, g] = sel[:, 256 + lo:256 + lo + DH].astype(BF16)
        kw_ref[0, g] = knorm(win[:, lo:lo + DH], kg_ref[2:3, :])
        vw_ref[0, g] = win[:, 256 + lo:256 + lo + DH].astype(BF16)
        gs_ref[0, g] = sig[:, g * 12:(g + 1) * 12]


def _nsa_prep(u, k_gain, b, s, ts):
    nt = s // ts
    kv_spec = pl.BlockSpec((1, NSA_G, ts, DH), lambda bi, i: (bi, 0, i, 0))
    kv_shape = jax.ShapeDtypeStruct((b, NSA_G, s, DH), BF16)
    return pl.pallas_call(
        _prep_kernel,
        grid=(b, nt),
        in_specs=[
            pl.BlockSpec((ts, 512), lambda bi, i: (bi * nt + i, COL_KV // 512 + 1)),
            pl.BlockSpec((ts, 512), lambda bi, i: (bi * nt + i, COL_KV // 512 + 2)),
            pl.BlockSpec((ts, MISC_W), lambda bi, i: (bi * nt + i, COL_MISC // MISC_W)),
            pl.BlockSpec((3, DH), lambda bi, i: (0, 0)),
        ],
        out_specs=[kv_spec, kv_spec, kv_spec, kv_spec,
                   pl.BlockSpec((1, NSA_G, ts, 12), lambda bi, i: (bi, 0, i, 0))],
        out_shape=[kv_shape, kv_shape, kv_shape, kv_shape,
                   jax.ShapeDtypeStruct((b, NSA_G, s, 12), F32)],
        compiler_params=_cparams(("arbitrary", "arbitrary")),
        name="nsa_prep",
    )(u, u, u, k_gain)


def _cmp_kernel(k01_ref, k23_ref, v01_ref, v23_ref, pek_ref, pev_ref, wk1_ref, wk2_ref,
                wv1_ref, wv2_ref, kg_ref, kc_ref, vc_ref, *, nseg):
    half = (CMP_BLOCK // 2) * DH

    def compress(src_ref, col0, pe_ref, w1_ref, w2_ref):
        segs = [src_ref[pl.ds(l, nseg, stride=CMP_STRIDE), :][:, col0:col0 + DH]
                for l in range(CMP_STRIDE)]
        xcat = jnp.concatenate(segs, axis=-1)
        ha = _dot((xcat + pe_ref[0:1, :]).astype(BF16), w1_ref[0:half, :])
        hb = _dot((xcat + pe_ref[1:2, :]).astype(BF16), w1_ref[half:2 * half, :])
        hb_next = jnp.concatenate([hb[1:, :], jnp.zeros((1, CMP_HIDDEN), F32)], axis=0)
        hid = ha + hb_next
        return _dot(_silu(hid).astype(BF16), w2_ref[...])

    for g in range(NSA_G):
        k_src = k01_ref if g < 2 else k23_ref
        v_src = v01_ref if g < 2 else v23_ref
        col0 = (g % 2) * DH
        kc = compress(k_src, col0, pek_ref, wk1_ref, wk2_ref)
        ms = jnp.mean(kc * kc, axis=-1, keepdims=True)
        kc_ref[0, g] = ((kc * lax.rsqrt(ms + EPS)) * kg_ref[0:1, :]).astype(BF16)
        vc_ref[0, g] = compress(v_src, col0, pev_ref, wv1_ref, wv2_ref).astype(BF16)


def _nsa_cmp(u, pek2, pev2, wk1, wk2, wv1, wv2, k_gain, b, s):
    nseg = s // CMP_STRIDE
    full = lambda shape: pl.BlockSpec(shape, lambda bi: tuple(0 for _ in shape))
    slab = lambda k: pl.BlockSpec((s, 128), lambda bi: (bi, COL_KV // 128 + k))
    out_spec = pl.BlockSpec((1, NSA_G, nseg, DH), lambda bi: (bi, 0, 0, 0))
    out_shape = jax.ShapeDtypeStruct((b, NSA_G, nseg, DH), BF16)
    return pl.pallas_call(
        functools.partial(_cmp_kernel, nseg=nseg),
        grid=(b,),
        in_specs=[
            slab(0), slab(1), slab(2), slab(3),
            full(pek2.shape), full(pev2.shape), full(wk1.shape), full(wk2.shape),
            full(wv1.shape), full(wv2.shape), full(k_gain.shape),
        ],
        out_specs=[out_spec, out_spec],
        out_shape=[out_shape, out_shape],
        compiler_params=_cparams(("arbitrary",)),
        name="nsa_cmp",
    )(u, u, u, u, pek2, pev2, wk1, wk2, wv1, wv2, k_gain)


def _softmax_rows(logit):
    m = jnp.max(logit, axis=-1, keepdims=True)
    e = jnp.exp(logit - m)
    return e * (1.0 / jnp.sum(e, axis=-1, keepdims=True))


def _nsa_kernel(slopes_ref, q_ref, gs_ref, kc_ref, vc_ref, ks_ref, vs_ref, kw_ref, vw_ref,
                qg_ref, ovl_ref, exp_ref, o_ref, *, tq, ck, seq):
    g = pl.program_id(1)
    i = pl.program_id(2)
    q0 = i * tq
    rows = NSA_R * tq
    nseg = seq // CMP_STRIDE
    ncmp = (seq - CMP_BLOCK) // CMP_STRIDE + 1
    nb = seq // SEL_BLOCK
    n_sel = min(SEL_TOPK, nb)
    wk = tq + WINDOW

    q = q_ref[...]
    qrows = []
    for r in range(NSA_R):
        qh = q[:, r * DH:(r + 1) * DH]
        ms = jnp.mean(qh * qh, axis=-1, keepdims=True)
        qrows.append(((qh * lax.rsqrt(ms + EPS)) * qg_ref[...]) * (DH ** -0.5))
    qs = jnp.concatenate(qrows, axis=0).astype(BF16)

    slope = jnp.concatenate(
        [jnp.full((tq, 1), slopes_ref[g * NSA_R + r], F32) for r in range(NSA_R)], axis=0)
    tq_col = q0 + lax.broadcasted_iota(I32, (tq, 1), 0)
    t_col = jnp.concatenate([tq_col] * NSA_R, axis=0)

    s_c = _dot_nt(qs, kc_ref[0, 0])
    cidx = lax.broadcasted_iota(I32, (1, nseg), 1)
    d_c = t_col - (cidx * CMP_STRIDE + (CMP_BLOCK - 1))
    vis_c = (d_c >= 0) & (cidx < ncmp)
    p_c = _softmax_rows(jnp.where(vis_c, s_c - slope * d_c.astype(F32), NEG_INF))
    p_c = p_c * (t_col >= CMP_BLOCK - 1).astype(F32)
    o_c = _dot(p_c.astype(BF16), vc_ref[0, 0])

    p_sum = (p_c[0:tq] + p_c[tq:2 * tq]) + (p_c[2 * tq:3 * tq] + p_c[3 * tq:4 * tq])
    imp = _dot_exact_r(p_sum, ovl_ref[...])

    cur = lax.shift_right_logical(tq_col, 6)
    jb = lax.broadcasted_iota(I32, (tq, nb), 1)
    forced = (jb == 0) | (jb == cur) | (jb == cur - 1)
    causal_blk = jb <= cur
    score = jnp.where(forced, jnp.inf, jnp.where(causal_blk, imp, -jnp.inf))
    rank = jnp.zeros((tq, nb), I32)
    for jp in range(nb):
        sj = score[:, jp:jp + 1]
        beats = (sj > score) | ((sj == score) & (jb > jp))
        rank = rank + beats.astype(I32)
    sel = jnp.where((rank < n_sel) & causal_blk, 1.0, 0.0).astype(BF16)

    def sel_chunk(c, carry):
        m_prev, l_prev, acc = carry
        k0 = pl.multiple_of(c * ck, ck)
        s_s = _dot_nt(qs, ks_ref[0, 0, pl.ds(k0, ck), :])
        selk = _dot(sel, exp_ref[c])
        selk4 = jnp.concatenate([selk] * NSA_R, axis=0)
        d_s = t_col - (k0 + lax.broadcasted_iota(I32, (1, ck), 1))
        ok = (selk4 > 0.5) & (d_s >= 0)
        logit = jnp.where(ok, s_s - slope * d_s.astype(F32), NEG_INF)
        m_new = jnp.maximum(m_prev, jnp.max(logit, axis=-1, keepdims=True))
        alpha = jnp.exp(m_prev - m_new)
        e = jnp.where(ok, jnp.exp(logit - m_new), 0.0)
        l_new = alpha * l_prev + jnp.sum(e, axis=-1, keepdims=True)
        acc = alpha * acc + _dot(e.astype(BF16), vs_ref[0, 0, pl.ds(k0, ck), :])
        return m_new, l_new, acc

    n_chunks = (q0 + tq + ck - 1) // ck
    init = (jnp.full((rows, 1), NEG_INF, F32), jnp.zeros((rows, 1), F32),
            jnp.zeros((rows, DH), F32))
    _, l_s, acc_s = lax.fori_loop(0, n_chunks, sel_chunk, init)
    o_s = acc_s * (1.0 / l_s)

    w0 = pl.multiple_of(jnp.maximum(q0 - WINDOW, 0), tq)
    s_w = _dot_nt(qs, kw_ref[0, 0, pl.ds(w0, wk), :])
    d_w = t_col - (w0 + lax.broadcasted_iota(I32, (1, wk), 1))
    vis_w = (d_w >= 0) & (d_w < WINDOW)
    p_w = _softmax_rows(jnp.where(vis_w, s_w - slope * d_w.astype(F32), NEG_INF))
    o_w = _dot(p_w.astype(BF16), vw_ref[0, 0, pl.ds(w0, wk), :])

    gs = gs_ref[0, 0]
    outs = []
    for r in range(NSA_R):
        sl = slice(r * tq, (r + 1) * tq)
        outs.append(gs[:, 3 * r:3 * r + 1] * o_c[sl]
                    + gs[:, 3 * r + 1:3 * r + 2] * o_s[sl]
                    + gs[:, 3 * r + 2:3 * r + 3] * o_w[sl])
    o_ref[...] = jnp.concatenate(outs, axis=-1).astype(o_ref.dtype)


def _nsa_attn(slopes, u, gs, kc, vc, ks, vs, kw, vw, q_gain, ovl, expand, b, s, tq, ck):
    nq = s // tq
    nseg = s // CMP_STRIDE
    nb = s // SEL_BLOCK
    per_bg = lambda rows: pl.BlockSpec((1, 1, rows, DH), lambda bi, g, i: (bi, g, 0, 0))
    return pl.pallas_call(
        functools.partial(_nsa_kernel, tq=tq, ck=ck, seq=s),
        grid=(b, NSA_G, nq),
        in_specs=[
            pl.BlockSpec(memory_space=pltpu.SMEM),
            pl.BlockSpec((tq, NSA_R * DH), lambda bi, g, i: (bi * nq + i, COL_Q // 256 + g)),
            pl.BlockSpec((1, 1, tq, 12), lambda bi, g, i: (bi, g, i, 0)),
            per_bg(nseg), per_bg(nseg), per_bg(s), per_bg(s), per_bg(s), per_bg(s),
            pl.BlockSpec((1, DH), lambda bi, g, i: (0, 0)),
            pl.BlockSpec((nseg, nb), lambda bi, g, i: (0, 0)),
            pl.BlockSpec((s // ck, nb, ck), lambda bi, g, i: (0, 0, 0)),
        ],
        out_specs=pl.BlockSpec((tq, NSA_R * DH), lambda bi, g, i: (bi * nq + i, g)),
        out_shape=jax.ShapeDtypeStruct((b * s, NSA_HEADS * DH), BF16),
        compiler_params=_cparams(("arbitrary", "arbitrary", "arbitrary")),
        name="nsa_attn",
    )(slopes, u, gs, kc, vc, ks, vs, kw, vw, q_gain, ovl, expand)


def _ssd_kernel(z_ref, xs_ref, bm_ref, cm_ref, misc_ref, cwx_ref, cwb_ref, cwc_ref,
                cbx_ref, cbb_ref, cbc_ref, dtb_ref, aneg_ref, dx_ref, nw_ref, tri_ref, e_ref,
                o_ref, bx_ref, bb_ref, bc_ref, st_ref):
    c = pl.program_id(1)
    L = SSM_L
    tail = SSM_CONV - 1
    pad = 8
    gw = SSM_E * SSM_P

    def conv_silu(buf_ref, cur_ref, w_ref, bias_ref):
        @pl.when(c == 0)
        def _():
            buf_ref[0:pad, :] = jnp.zeros((pad, buf_ref.shape[1]), F32)

        @pl.when(c > 0)
        def _():
            buf_ref[0:pad, :] = buf_ref[L:L + pad, :]

        buf_ref[pad:pad + L, :] = cur_ref[...]
        acc = bias_ref[...]
        for k in range(SSM_CONV):
            acc = acc + w_ref[k:k + 1, :] * buf_ref[pad - tail + k:pad - tail + k + L, :]
        return _silu(acc)

    xs = conv_silu(bx_ref, xs_ref, cwx_ref, cbx_ref)
    bm = conv_silu(bb_ref, bm_ref, cwb_ref, cbb_ref)
    cm = conv_silu(bc_ref, cm_ref, cwc_ref, cbc_ref)

    @pl.when(c == 0)
    def _():
        st_ref[...] = jnp.zeros(st_ref.shape, F32)

    lane = lax.broadcasted_iota(I32, (1, MISC_W), 1)
    head_lane = (lane >= DT_OFF) & (lane < DT_OFF + SSM_HEADS)
    xdt_in = misc_ref[...] + dtb_ref[...]
    dt = jnp.where(head_lane, jnp.maximum(xdt_in, 0.0) + jnp.log1p(jnp.exp(-jnp.abs(xdt_in))), 0.0)
    adt = dt * aneg_ref[...]
    acum = _dot_exact_l(tri_ref[...], adt)
    acum_t = acum.T
    acum_x = _dot_exact_r(acum, e_ref[...])
    dt_x = _dot_exact_r(dt, e_ref[...])
    expa_x = jnp.exp(acum_x)
    dstate_x = jnp.exp(acum_x[L - 1:L, :] - acum_x)
    xdt = xs * dt_x
    xdt_b = xdt.astype(BF16)
    xw_b = (xdt * dstate_x).astype(BF16)

    li = lax.broadcasted_iota(I32, (L, L), 0)
    si = lax.broadcasted_iota(I32, (L, L), 1)
    lower = li >= si

    for g in range(SSM_G):
        gs = slice(g * gw, (g + 1) * gw)
        ns = slice(g * SSM_N, (g + 1) * SSM_N)
        bm_g = bm[:, ns]
        cm_g = cm[:, ns].astype(BF16)
        cb = _dot_nt(cm_g, bm_g.astype(BF16))
        ys = []
        for e in range(SSM_E):
            h = g * SSM_E + e
            col = acum[:, DT_OFF + h:DT_OFF + h + 1]
            row = acum_t[DT_OFF + h:DT_OFF + h + 1, :]
            lmat = jnp.exp(jnp.where(lower, col - row, NEG_INF))
            ys.append(_dot((cb * lmat).astype(BF16), xdt_b[:, h * SSM_P:(h + 1) * SSM_P]))
        y_diag = jnp.concatenate(ys, axis=-1)
        prev = st_ref[g]
        y_off = _dot(cm_g, prev.astype(BF16)) * expa_x[:, gs]
        st_new = _dot(bm_g.T.astype(BF16), xw_b[:, gs])
        st_ref[g] = expa_x[L - 1:L, gs] * prev + st_new
        y = y_diag + y_off + dx_ref[:, gs] * xs[:, gs]
        y = y * _silu(z_ref[:, gs])
        ms = jnp.mean(y * y, axis=-1, keepdims=True)
        o_ref[:, gs] = ((y * lax.rsqrt(ms + EPS)) * nw_ref[:, gs]).astype(o_ref.dtype)


def _ssd(u, cwx, cwb, cwc, cbx, cbb, cbc, dtb, aneg, dx, nw, tri, expand, b, s):
    nc = s // SSM_L
    L = SSM_L
    row = lambda width, col: pl.BlockSpec((L, width), lambda bi, c: (bi * nc + c, col))
    full = lambda a: pl.BlockSpec(a.shape, lambda bi, c: tuple(0 for _ in a.shape))
    consts = (cwx, cwb, cwc, cbx, cbb, cbc, dtb, aneg, dx, nw, tri, expand)
    return pl.pallas_call(
        _ssd_kernel,
        grid=(b, nc),
        in_specs=[
            row(SSM_D_INNER, COL_Z // SSM_D_INNER),
            row(SSM_D_INNER, COL_XS // SSM_D_INNER),
            row(512, COL_BM // 512),
            row(512, COL_CM // 512),
            row(MISC_W, COL_MISC // MISC_W),
        ] + [full(a) for a in consts],
        out_specs=pl.BlockSpec((L, SSM_D_INNER), lambda bi, c: (bi * nc + c, 0)),
        out_shape=jax.ShapeDtypeStruct((b * s, SSM_D_INNER), BF16),
        scratch_shapes=[
            pltpu.VMEM((L + 8, SSM_D_INNER), F32),
            pltpu.VMEM((L + 8, 512), F32),
            pltpu.VMEM((L + 8, 512), F32),
            pltpu.VMEM((SSM_G, SSM_N, SSM_E * SSM_P), F32),
        ],
        compiler_params=_cparams(("arbitrary", "arbitrary")),
        name="ssd",
    )(u, u, u, u, u, *consts)


def _merge_kernel(x_ref, ya_ref, yb_ref, ga_ref, gb_ref, wa_ref, wb_ref, wo_ref, o_ref):
    pa = _dot(ya_ref[...], wa_ref[...])
    pb = _dot(yb_ref[...], wb_ref[...])
    merged = _sigmoid(ga_ref[...]) * pa + _sigmoid(gb_ref[...]) * pb
    o_ref[...] = x_ref[...] + _dot(merged.astype(BF16), wo_ref[...])


def _merge(x2, ya, yb, u, wa, wb, wo, tm):
    t, d = x2.shape
    full = lambda a: pl.BlockSpec(a.shape, lambda i: (0, 0))
    return pl.pallas_call(
        _merge_kernel,
        grid=(t // tm,),
        in_specs=[
            pl.BlockSpec((tm, d), lambda i: (i, 0)),
            pl.BlockSpec((tm, ya.shape[1]), lambda i: (i, 0)),
            pl.BlockSpec((tm, yb.shape[1]), lambda i: (i, 0)),
            pl.BlockSpec((tm, d), lambda i: (i, COL_GA // D_MODEL)),
            pl.BlockSpec((tm, d), lambda i: (i, COL_GB // D_MODEL)),
            full(wa), full(wb), full(wo),
        ],
        out_specs=pl.BlockSpec((tm, d), lambda i: (i, 0)),
        out_shape=jax.ShapeDtypeStruct((t, d), F32),
        compiler_params=_cparams(("arbitrary",)),
        name="merge",
    )(x2, ya, yb, u, u, wa, wb, wo)


def _ffn_kernel(x_ref, nw_ref, wg_ref, wv_ref, cwg_ref, cwv_ref, cbg_ref, cbv_ref, wd_ref,
                o_ref, h_ref, acc_ref, bg_ref, bv_ref, cg_ref, cv_ref, *, tm, tiles_per_seq):
    i = pl.program_id(0)
    f = pl.program_id(1)
    nf = pl.num_programs(1)
    pad = 8
    tail = FFN_CONV - 1
    seq_start = (i % tiles_per_seq) == 0

    @pl.when(f == 0)
    def _():
        x = x_ref[...]
        ms = jnp.mean(x * x, axis=-1, keepdims=True)
        h_ref[...] = ((x * lax.rsqrt(ms + EPS)) * nw_ref[...]).astype(BF16)
        acc_ref[...] = jnp.zeros(acc_ref.shape, F32)

    def conv(buf_ref, carry_ref, w_ref, cw_ref, cb_ref):
        @pl.when(seq_start)
        def _():
            buf_ref[0:pad, :] = jnp.zeros((pad, buf_ref.shape[1]), F32)

        @pl.when(jnp.logical_not(seq_start))
        def _():
            buf_ref[0:pad, :] = carry_ref[f]

        buf_ref[pad:pad + tm, :] = _dot(h_ref[...], w_ref[...])
        carry_ref[f] = buf_ref[tm:tm + pad, :]
        acc = cb_ref[...]
        for k in range(FFN_CONV):
            acc = acc + cw_ref[k:k + 1, :] * buf_ref[pad - tail + k:pad - tail + k + tm, :]
        return acc

    gate = conv(bg_ref, cg_ref, wg_ref, cwg_ref, cbg_ref)
    val = conv(bv_ref, cv_ref, wv_ref, cwv_ref, cbv_ref)
    acc_ref[...] += _dot((_silu(gate) * val).astype(BF16), wd_ref[...])

    @pl.when(f == nf - 1)
    def _():
        o_ref[...] = x_ref[...] + acc_ref[...]


def _ffn(x1, nw, w_up, conv_w, conv_b, w_down, s, tm, tf):
    t, d = x1.shape
    nf = FFN_HIDDEN // tf
    return pl.pallas_call(
        functools.partial(_ffn_kernel, tm=tm, tiles_per_seq=s // tm),
        grid=(t // tm, nf),
        in_specs=[
            pl.BlockSpec((tm, d), lambda i, f: (i, 0)),
            pl.BlockSpec((1, d), lambda i, f: (0, 0)),
            pl.BlockSpec((d, tf), lambda i, f: (0, f)),
            pl.BlockSpec((d, tf), lambda i, f: (0, nf + f)),
            pl.BlockSpec((FFN_CONV, tf), lambda i, f: (0, f)),
            pl.BlockSpec((FFN_CONV, tf), lambda i, f: (0, nf + f)),
            pl.BlockSpec((1, tf), lambda i, f: (0, f)),
            pl.BlockSpec((1, tf), lambda i, f: (0, nf + f)),
            pl.BlockSpec((tf, d), lambda i, f: (f, 0)),
        ],
        out_specs=pl.BlockSpec((tm, d), lambda i, f: (i, 0)),
        out_shape=jax.ShapeDtypeStruct((t, d), F32),
        scratch_shapes=[
            pltpu.VMEM((tm, d), BF16),
            pltpu.VMEM((tm, d), F32),
            pltpu.VMEM((tm + 8, tf), F32),
            pltpu.VMEM((tm + 8, tf), F32),
            pltpu.VMEM((nf, 8, tf), F32),
            pltpu.VMEM((nf, 8, tf), F32),
        ],
        compiler_params=_cparams(("arbitrary", "arbitrary")),
        name="ffn",
    )(x1, nw, w_up, w_up, conv_w, conv_w, conv_b, conv_b, w_down)


def _regroup_w_in(w):
    o = 0
    parts = {}
    for name, width in (("q", 1024), ("kv", 1536), ("gn", 48), ("z", 2048), ("xs", 2048),
                        ("bm", 512), ("cm", 512), ("dt", 32), ("ga", 1024), ("gb", 1024)):
        parts[name] = w[:, o:o + width]
        o += width
    pad = jnp.zeros((w.shape[0], MISC_W - 48 - 32), w.dtype)
    return jnp.concatenate([parts["z"], parts["xs"], parts["q"], parts["ga"], parts["gb"],
                            parts["bm"], parts["cm"], parts["kv"], parts["gn"], parts["dt"], pad],
                           axis=1).astype(BF16)


def _layer(x, norm1_w, w_in, q_gain, k_gain, pe_k, pe_v, wk1, wk2, wv1, wv2, conv_w, conv_b,
           dt_bias, a_log, d_skip, ssm_norm_w, w_proj_a, w_proj_b, w_out, norm2_w, w_up,
           ffn_conv_w, ffn_conv_b, w_down):
    b, s, d = x.shape
    t = b * s
    x2 = x.reshape(t, d)
    tm = min(1024, s)

    u = _in_proj(x2, norm1_w.reshape(1, d), _regroup_w_in(w_in), tm, IN_W_PAD // 7)

    ks, vs, kw, vw, gs = _nsa_prep(u, k_gain, b, s, min(512, s))
    half = (CMP_BLOCK // 2)
    kc, vc = _nsa_cmp(u, pe_k.reshape(2, half * DH), pe_v.reshape(2, half * DH),
                      wk1.astype(BF16), wk2.astype(BF16), wv1.astype(BF16), wv2.astype(BF16),
                      k_gain, b, s)
    nseg = s // CMP_STRIDE
    nb = s // SEL_BLOCK
    tq, ck = 128, 512
    cstart = jnp.arange(nseg) * CMP_STRIDE
    bstart = jnp.arange(nb) * SEL_BLOCK
    ncmp = (s - CMP_BLOCK) // CMP_STRIDE + 1
    ovl = ((cstart[:, None] <= bstart[None, :] + SEL_BLOCK - 1)
           & (cstart[:, None] + CMP_BLOCK - 1 >= bstart[None, :])
           & (jnp.arange(nseg)[:, None] < ncmp)).astype(BF16)
    pos = jnp.arange(s).reshape(s // ck, 1, ck)
    expand = (pos // SEL_BLOCK == jnp.arange(nb)[None, :, None]).astype(BF16)
    slopes = jnp.exp2(-8.0 * jnp.arange(1, NSA_HEADS + 1, dtype=F32) / NSA_HEADS)
    ya = _nsa_attn(slopes, u, gs, kc, vc, ks, vs, kw, vw, q_gain.reshape(1, DH), ovl, expand,
                   b, s, tq, ck)

    lanes = jnp.arange(MISC_W)
    in_heads = (lanes >= DT_OFF) & (lanes < DT_OFF + SSM_HEADS)
    hidx = jnp.clip(lanes - DT_OFF, 0, SSM_HEADS - 1)
    dtb = jnp.where(in_heads, dt_bias[hidx], 0.0).reshape(1, MISC_W)
    aneg = jnp.where(in_heads, -jnp.exp(a_log.astype(F32))[hidx], 0.0).reshape(1, MISC_W)
    e_heads = ((lanes[:, None] - DT_OFF) == (jnp.arange(SSM_D_INNER)[None, :] // SSM_P)).astype(BF16)
    tri = (jnp.arange(SSM_L)[:, None] >= jnp.arange(SSM_L)[None, :]).astype(BF16)
    dx = jnp.repeat(d_skip.astype(F32), SSM_P).reshape(1, SSM_D_INNER)
    yb = _ssd(u, conv_w[:, :2048], conv_w[:, 2048:2560], conv_w[:, 2560:],
              conv_b[:2048].reshape(1, -1), conv_b[2048:2560].reshape(1, -1),
              conv_b[2560:].reshape(1, -1), dtb, aneg, dx, ssm_norm_w.reshape(1, -1), tri,
              e_heads, b, s)

    x1 = _merge(x2, ya, yb, u, w_proj_a.astype(BF16), w_proj_b.astype(BF16), w_out.astype(BF16),
                min(512, s))
    out = _ffn(x1, norm2_w.reshape(1, d), w_up.astype(BF16), ffn_conv_w, ffn_conv_b.reshape(1, -1),
               w_down.astype(BF16), s, min(512, s), 256)
    return out.reshape(b, s, d)


def kernel(x, norm1_w, w_in, nsa_q_gain, nsa_k_gain, cmp_pe_k, cmp_pe_v, cmp_wk1, cmp_wk2, cmp_wv1,
           cmp_wv2, ssm_conv_w, ssm_conv_b, ssm_dt_bias, ssm_a_log, ssm_d, ssm_norm_w, w_proj_a,
           w_proj_b, w_out, norm2_w, ffn_w_up, ffn_conv_w, ffn_conv_b, ffn_w_down):
    for layer in range(norm1_w.shape[0]):
        x = _layer(x, norm1_w[layer], w_in[layer], nsa_q_gain[layer], nsa_k_gain[layer],
                   cmp_pe_k[layer], cmp_pe_v[layer], cmp_wk1[layer], cmp_wk2[layer],
                   cmp_wv1[layer], cmp_wv2[layer], ssm_conv_w[layer], ssm_conv_b[layer],
                   ssm_dt_bias[layer], ssm_a_log[layer], ssm_d[layer], ssm_norm_w[layer],
                   w_proj_a[layer], w_proj_b[layer], w_out[layer], norm2_w[layer],
                   ffn_w_up[layer], ffn_conv_w[layer], ffn_conv_b[layer], ffn_w_down[layer])
    return x
```

```python
import functools

import jax
import jax.numpy as jnp
from jax import lax
from jax.experimental import pallas as pl
from jax.experimental.pallas import tpu as pltpu

F32 = jnp.float32
BF16 = jnp.bfloat16
I32 = jnp.int32

D_MODEL = 1024
NSA_HEADS = 16
NSA_G = 4
NSA_R = 4
DH = 64
CMP_BLOCK = 32
CMP_STRIDE = 16
CMP_HIDDEN = 256
SEL_BLOCK = 64
SEL_TOPK = 8
WINDOW = 512
SSM_D_INNER = 2048
SSM_P = 64
SSM_HEADS = 32
SSM_G = 4
SSM_E = SSM_HEADS // SSM_G
SSM_N = 128
SSM_CONV = 4
SSM_L = 128
FFN_HIDDEN = 2816
FFN_CONV = 3
EPS = 1e-6
NEG_INF = -1e30

COL_Z = 0
COL_XS = 2048
COL_Q = 4096
COL_GA = 5120
COL_GB = 6144
COL_BM = 7168
COL_CM = 7680
COL_KV = 8192
COL_MISC = 9728
IN_W_PAD = 9856
MISC_W = 128
DT_OFF = 3 * NSA_HEADS

AUG_W = 64
AUG_POS = 32
LOG2E = 1.4426950408889634
MASK_BIG = 1e30

VMEM_LIMIT = 56 * 1024 * 1024


def _cparams(sem):
    return pltpu.CompilerParams(dimension_semantics=sem, vmem_limit_bytes=VMEM_LIMIT)


def _sigmoid(x):
    return 1.0 / (1.0 + jnp.exp(-x))


def _silu(x):
    return x * _sigmoid(x)


def _split3(a):
    hi = a.astype(BF16)
    r1 = a - hi.astype(F32)
    mid = r1.astype(BF16)
    lo = (r1 - mid.astype(F32)).astype(BF16)
    return hi, mid, lo


def _dot(a, b):
    return jnp.dot(a, b, preferred_element_type=F32)


def _dot_nt(a, b):
    return lax.dot_general(a, b, (((1,), (1,)), ((), ())), preferred_element_type=F32)


def _dot_exact_r(a, m):
    hi, mid, lo = _split3(a)
    return _dot(hi, m) + _dot(mid, m) + _dot(lo, m)


def _dot_exact_l(m, a):
    hi, mid, lo = _split3(a)
    return _dot(m, hi) + _dot(m, mid) + _dot(m, lo)


def _inproj_kernel(x_ref, nw_ref, w_ref, o_ref, h_ref):
    @pl.when(pl.program_id(1) == 0)
    def _():
        x = x_ref[...]
        ms = jnp.mean(x * x, axis=-1, keepdims=True)
        h_ref[...] = ((x * lax.rsqrt(ms + EPS)) * nw_ref[...]).astype(BF16)

    o_ref[...] = _dot(h_ref[...], w_ref[...])


def _in_proj(x2, nw, w, tm, tn):
    t, d = x2.shape
    n = w.shape[1]
    return pl.pallas_call(
        _inproj_kernel,
        grid=(t // tm, n // tn),
        in_specs=[
            pl.BlockSpec((tm, d), lambda i, j: (i, 0)),
            pl.BlockSpec((1, d), lambda i, j: (0, 0)),
            pl.BlockSpec((d, tn), lambda i, j: (0, j)),
        ],
        out_specs=pl.BlockSpec((tm, tn), lambda i, j: (i, j)),
        out_shape=jax.ShapeDtypeStruct((t, n), F32),
        scratch_shapes=[pltpu.VMEM((tm, d), BF16)],
        compiler_params=_cparams(("arbitrary", "arbitrary")),
        name="in_proj",
    )(x2, nw, w)


def _pos_aug(pos):
    n = pos.shape[0]
    lane = lax.broadcasted_iota(I32, (n, AUG_W), 1)
    blk = lax.shift_right_logical(pos, 6)
    piece = jnp.where((lane & 1) == 0, blk * SEL_BLOCK, pos & (SEL_BLOCK - 1)).astype(F32)
    in_pos = (lane >= AUG_POS) & (lane < AUG_POS + 6)
    return jnp.where(in_pos, piece, jnp.where(lane == blk, 1.0, 0.0))


def _prep_kernel(sel_ref, win_ref, misc_ref, kg_ref, ks_ref, vs_ref, kw_ref, vw_ref, gs_ref, *, ts):
    def knorm(k, gain):
        ms = jnp.mean(k * k, axis=-1, keepdims=True)
        return (k * lax.rsqrt(ms + EPS)) * gain

    sel = sel_ref[...]
    win = win_ref[...]
    sig = _sigmoid(misc_ref[...])
    pos = pl.program_id(1) * ts + lax.broadcasted_iota(I32, (ts, 1), 0)
    aug = _pos_aug(pos)
    for g in range(NSA_G):
        lo = g * DH
        ks_ref[0, g] = jnp.concatenate(
            [knorm(sel[:, lo:lo + DH], kg_ref[1:2, :]), aug], axis=-1).astype(BF16)
        vs_ref[0, g] = sel[:, 256 + lo:256 + lo + DH].astype(BF16)
        kw_ref[0, g] = jnp.concatenate(
            [knorm(win[:, lo:lo + DH], kg_ref[2:3, :]), aug], axis=-1).astype(BF16)
        vw_ref[0, g] = win[:, 256 + lo:256 + lo + DH].astype(BF16)
        gs_ref[0, g] = sig[:, g * 12:(g + 1) * 12]


def _nsa_prep(u, k_gain, b, s, ts):
    nt = s // ts
    k_spec = pl.BlockSpec((1, NSA_G, ts, DH + AUG_W), lambda bi, i: (bi, 0, i, 0))
    k_shape = jax.ShapeDtypeStruct((b, NSA_G, s, DH + AUG_W), BF16)
    kv_spec = pl.BlockSpec((1, NSA_G, ts, DH), lambda bi, i: (bi, 0, i, 0))
    kv_shape = jax.ShapeDtypeStruct((b, NSA_G, s, DH), BF16)
    return pl.pallas_call(
        functools.partial(_prep_kernel, ts=ts),
        grid=(b, nt),
        in_specs=[
            pl.BlockSpec((ts, 512), lambda bi, i: (bi * nt + i, COL_KV // 512 + 1)),
            pl.BlockSpec((ts, 512), lambda bi, i: (bi * nt + i, COL_KV // 512 + 2)),
            pl.BlockSpec((ts, MISC_W), lambda bi, i: (bi * nt + i, COL_MISC // MISC_W)),
            pl.BlockSpec((3, DH), lambda bi, i: (0, 0)),
        ],
        out_specs=[k_spec, kv_spec, k_spec, kv_spec,
                   pl.BlockSpec((1, NSA_G, ts, 12), lambda bi, i: (bi, 0, i, 0))],
        out_shape=[k_shape, kv_shape, k_shape, kv_shape,
                   jax.ShapeDtypeStruct((b, NSA_G, s, 12), F32)],
        compiler_params=_cparams(("arbitrary", "arbitrary")),
        name="nsa_prep",
    )(u, u, u, k_gain)


def _cmp_kernel(k01_ref, k23_ref, v01_ref, v23_ref, pek_ref, pev_ref, wk1_ref, wk2_ref,
                wv1_ref, wv2_ref, kg_ref, kc_ref, vc_ref, *, nseg):
    half = (CMP_BLOCK // 2) * DH

    def compress(src_ref, col0, pe_ref, w1_ref, w2_ref):
        segs = [src_ref[pl.ds(l, nseg, stride=CMP_STRIDE), :][:, col0:col0 + DH]
                for l in range(CMP_STRIDE)]
        xcat = jnp.concatenate(segs, axis=-1)
        ha = _dot((xcat + pe_ref[0:1, :]).astype(BF16), w1_ref[0:half, :])
        hb = _dot((xcat + pe_ref[1:2, :]).astype(BF16), w1_ref[half:2 * half, :])
        hb_next = jnp.concatenate([hb[1:, :], jnp.zeros((1, CMP_HIDDEN), F32)], axis=0)
        hid = ha + hb_next
        return _dot(_silu(hid).astype(BF16), w2_ref[...])

    cend = lax.broadcasted_iota(I32, (nseg, 1), 0) * CMP_STRIDE + (CMP_BLOCK - 1)
    aug = _pos_aug(cend)
    for g in range(NSA_G):
        k_src = k01_ref if g < 2 else k23_ref
        v_src = v01_ref if g < 2 else v23_ref
        col0 = (g % 2) * DH
        kc = compress(k_src, col0, pek_ref, wk1_ref, wk2_ref)
        ms = jnp.mean(kc * kc, axis=-1, keepdims=True)
        kc = (kc * lax.rsqrt(ms + EPS)) * kg_ref[0:1, :]
        kc_ref[0, g] = jnp.concatenate([kc, aug], axis=-1).astype(BF16)
        vc_ref[0, g] = compress(v_src, col0, pev_ref, wv1_ref, wv2_ref).astype(BF16)


def _nsa_cmp(u, pek2, pev2, wk1, wk2, wv1, wv2, k_gain, b, s):
    nseg = s // CMP_STRIDE
    full = lambda shape: pl.BlockSpec(shape, lambda bi: tuple(0 for _ in shape))
    slab = lambda k: pl.BlockSpec((s, 128), lambda bi: (bi, COL_KV // 128 + k))
    out_spec = lambda w: pl.BlockSpec((1, NSA_G, nseg, w), lambda bi: (bi, 0, 0, 0))
    out_shape = lambda w: jax.ShapeDtypeStruct((b, NSA_G, nseg, w), BF16)
    return pl.pallas_call(
        functools.partial(_cmp_kernel, nseg=nseg),
        grid=(b,),
        in_specs=[
            slab(0), slab(1), slab(2), slab(3),
            full(pek2.shape), full(pev2.shape), full(wk1.shape), full(wk2.shape),
            full(wv1.shape), full(wv2.shape), full(k_gain.shape),
        ],
        out_specs=[out_spec(DH + AUG_W), out_spec(DH)],
        out_shape=[out_shape(DH + AUG_W), out_shape(DH)],
        compiler_params=_cparams(("arbitrary",)),
        name="nsa_cmp",
    )(u, u, u, u, pek2, pev2, wk1, wk2, wv1, wv2, k_gain)


def _nsa_kernel(sl_ref, q_ref, gs_ref, kc_ref, vc_ref, ks_ref, vs_ref, kw_ref, vw_ref,
                qg_ref, ovlt_ref, eye_ref, dtab_ref, wtab_ref, o_ref, *, tq, ck, seq):
    g = pl.program_id(1)
    i = pl.program_id(2)
    q0 = i * tq
    rows = NSA_R * tq
    nseg = seq // CMP_STRIDE
    ncmp = (seq - CMP_BLOCK) // CMP_STRIDE + 1
    nb = seq // SEL_BLOCK
    n_sel = min(SEL_TOPK, nb)
    wk = tq + WINDOW
    nbr = ovlt_ref.shape[0]
    per_ck = ck // tq

    q = q_ref[...]
    lane = lax.broadcasted_iota(I32, (tq, AUG_W), 1)
    qparts, cparts = [], []
    for r in range(NSA_R):
        qh = q[:, r * DH:(r + 1) * DH]
        ms = jnp.mean(qh * qh, axis=-1, keepdims=True)
        qparts.append(((qh * lax.rsqrt(ms + EPS)) * qg_ref[...]) * (DH ** -0.5 * LOG2E))
        h = g * NSA_R + r
        cparts.append(jnp.where(
            lane < AUG_POS, 0.0,
            jnp.where(lane < AUG_POS + 2, sl_ref[3 * h],
                      jnp.where(lane < AUG_POS + 4, sl_ref[3 * h + 1],
                                jnp.where(lane < AUG_POS + 6, sl_ref[3 * h + 2], 0.0)))))
    qs = jnp.concatenate(qparts, axis=0)
    cpart = jnp.concatenate(cparts, axis=0)
    qa_base = jnp.concatenate([qs, cpart], axis=-1).astype(BF16)

    tq_col = q0 + lax.broadcasted_iota(I32, (tq, 1), 0)
    t_col = jnp.concatenate([tq_col] * NSA_R, axis=0)

    s_c = _dot_nt(qa_base, kc_ref[0, 0])
    cidx = lax.broadcasted_iota(I32, (1, nseg), 1)
    vis_c = (t_col >= cidx * CMP_STRIDE + (CMP_BLOCK - 1)) & (cidx < ncmp)
    logit_c = jnp.where(vis_c, s_c, NEG_INF)
    e_c = jnp.exp2(logit_c - jnp.max(logit_c, axis=-1, keepdims=True))
    inv_c = (1.0 / jnp.sum(e_c, axis=-1, keepdims=True)) * (t_col >= CMP_BLOCK - 1).astype(F32)
    p_c = e_c * inv_c
    o_c = _dot(p_c.astype(BF16), vc_ref[0, 0])

    p_sum = (p_c[0:tq] + p_c[tq:2 * tq]) + (p_c[2 * tq:3 * tq] + p_c[3 * tq:4 * tq])
    hi, mid, lo = _split3(p_sum)
    ovlt = ovlt_ref[...]
    imp_t = _dot_nt(ovlt, hi) + _dot_nt(ovlt, mid) + _dot_nt(ovlt, lo)

    cur = lax.shift_right_logical(q0 + lax.broadcasted_iota(I32, (1, tq), 1), 6)
    jb = lax.broadcasted_iota(I32, (nbr, tq), 0)
    forced = (jb == 0) | (jb == cur) | (jb == cur - 1)
    causal_blk = jb <= cur
    score = jnp.where(forced, jnp.inf, jnp.where(causal_blk, imp_t, -jnp.inf))
    rank = jnp.zeros((nbr, tq), I32)
    for jp in range(nb):
        sj = score[jp:jp + 1, :]
        beats = (sj > score) | ((sj == score) & (jb > jp))
        rank = rank + beats.astype(I32)
    mask_t = jnp.where((rank < n_sel) & causal_blk, 0.0, -MASK_BIG)
    mask_t = jnp.concatenate([mask_t, jnp.zeros((AUG_W - nbr, tq), F32)], axis=0).astype(BF16)
    mask_q = _dot_nt(eye_ref[...], mask_t)
    qa_sel = jnp.concatenate(
        [qs, cpart + jnp.concatenate([mask_q] * NSA_R, axis=0)], axis=-1).astype(BF16)

    def sel_scores(c):
        k0 = pl.multiple_of(c * ck, ck)
        return _dot_nt(qa_sel, ks_ref[0, 0, pl.ds(k0, ck), :]), k0

    def online(s, k0, carry):
        m_prev, l_prev, acc = carry
        m_new = jnp.maximum(m_prev, jnp.max(s, axis=-1, keepdims=True))
        alpha = jnp.exp2(m_prev - m_new)
        e = jnp.exp2(s - m_new)
        l_new = alpha * l_prev + jnp.sum(e, axis=-1, keepdims=True)
        acc = alpha * acc + _dot(e.astype(BF16), vs_ref[0, 0, pl.ds(k0, ck), :])
        return m_new, l_new, acc

    def sel_chunk(c, carry):
        s, k0 = sel_scores(c)
        return online(s, k0, carry)

    c_last = i // per_ck
    init = (jnp.full((rows, 1), NEG_INF, F32), jnp.zeros((rows, 1), F32),
            jnp.zeros((rows, DH), F32))
    carry = lax.fori_loop(0, c_last, sel_chunk, init)
    s_last, k_last = sel_scores(c_last)
    s_last = s_last + jnp.concatenate([dtab_ref[lax.rem(i, per_ck)]] * NSA_R, axis=0)
    _, l_s, acc_s = online(s_last, k_last, carry)
    o_s = acc_s * (1.0 / l_s)

    w0 = pl.multiple_of(jnp.maximum(q0 - WINDOW, 0), tq)
    wtab = wtab_ref[jnp.minimum(i, WINDOW // tq)]
    s_w = _dot_nt(qa_base, kw_ref[0, 0, pl.ds(w0, wk), :]) + jnp.concatenate([wtab] * NSA_R, axis=0)
    e_w = jnp.exp2(s_w - jnp.max(s_w, axis=-1, keepdims=True))
    o_w = _dot(e_w.astype(BF16), vw_ref[0, 0, pl.ds(w0, wk), :])
    o_w = o_w * (1.0 / jnp.sum(e_w, axis=-1, keepdims=True))

    gs = gs_ref[0, 0]
    outs = []
    for r in range(NSA_R):
        sl = slice(r * tq, (r + 1) * tq)
        outs.append(gs[:, 3 * r:3 * r + 1] * o_c[sl]
                    + gs[:, 3 * r + 1:3 * r + 2] * o_s[sl]
                    + gs[:, 3 * r + 2:3 * r + 3] * o_w[sl])
    o_ref[...] = jnp.concatenate(outs, axis=-1).astype(o_ref.dtype)


def _nsa_tables(s, tq, ck):
    nseg = s // CMP_STRIDE
    nb = s // SEL_BLOCK
    nbr = AUG_POS
    ncmp = (s - CMP_BLOCK) // CMP_STRIDE + 1
    assert nb <= nbr and ck % tq == 0 and WINDOW % tq == 0
    cstart = jnp.arange(nseg) * CMP_STRIDE
    bstart = jnp.arange(nbr) * SEL_BLOCK
    ovlt = ((cstart[None, :] <= bstart[:, None] + SEL_BLOCK - 1)
            & (cstart[None, :] + CMP_BLOCK - 1 >= bstart[:, None])
            & (jnp.arange(nseg)[None, :] < ncmp)
            & (jnp.arange(nbr)[:, None] < nb)).astype(BF16)
    eye = jnp.eye(tq, dtype=BF16)
    qi = jnp.arange(tq)[None, :, None]
    kj = jnp.arange(ck)[None, None, :]
    v = jnp.arange(ck // tq)[:, None, None]
    dtab = jnp.where(kj > v * tq + qi, -MASK_BIG, 0.0).astype(F32)
    wk = tq + WINDOW
    kj = jnp.arange(wk)[None, None, :]
    v = jnp.arange(WINDOW // tq)[:, None, None]
    early = kj > v * tq + qi
    late = jnp.logical_not((kj > qi[0:1]) & (kj <= qi[0:1] + WINDOW))
    wtab = jnp.where(jnp.concatenate([early, late], axis=0), -MASK_BIG, 0.0).astype(F32)
    return ovlt, eye, dtab, wtab


def _nsa_attn(slope_pieces, u, gs, kc, vc, ks, vs, kw, vw, q_gain, b, s, tq, ck):
    nq = s // tq
    nseg = s // CMP_STRIDE
    ovlt, eye, dtab, wtab = _nsa_tables(s, tq, ck)
    per_bg = lambda rows, w: pl.BlockSpec((1, 1, rows, w), lambda bi, g, i: (bi, g, 0, 0))
    const = lambda a: pl.BlockSpec(a.shape, lambda bi, g, i: tuple(0 for _ in a.shape))
    ka = DH + AUG_W
    return pl.pallas_call(
        functools.partial(_nsa_kernel, tq=tq, ck=ck, seq=s),
        grid=(b, NSA_G, nq),
        in_specs=[
            pl.BlockSpec(memory_space=pltpu.SMEM),
            pl.BlockSpec((tq, NSA_R * DH), lambda bi, g, i: (bi * nq + i, COL_Q // 256 + g)),
            pl.BlockSpec((1, 1, tq, 12), lambda bi, g, i: (bi, g, i, 0)),
            per_bg(nseg, ka), per_bg(nseg, DH), per_bg(s, ka), per_bg(s, DH),
            per_bg(s, ka), per_bg(s, DH),
            pl.BlockSpec((1, DH), lambda bi, g, i: (0, 0)),
            const(ovlt), const(eye), const(dtab), const(wtab),
        ],
        out_specs=pl.BlockSpec((tq, NSA_R * DH), lambda bi, g, i: (bi * nq + i, g)),
        out_shape=jax.ShapeDtypeStruct((b * s, NSA_HEADS * DH), BF16),
        compiler_params=_cparams(("arbitrary", "arbitrary", "arbitrary")),
        name="nsa_attn",
    )(slope_pieces, u, gs, kc, vc, ks, vs, kw, vw, q_gain, ovlt, eye, dtab, wtab)


def _ssd_kernel(z_ref, xs_ref, bm_ref, cm_ref, misc_ref, cwx_ref, cwb_ref, cwc_ref,
                cbx_ref, cbb_ref, cbc_ref, dtb_ref, aneg_ref, dx_ref, nw_ref, tri_ref, e_ref,
                o_ref, bx_ref, bb_ref, bc_ref, st_ref):
    c = pl.program_id(1)
    L = SSM_L
    tail = SSM_CONV - 1
    pad = 8
    gw = SSM_E * SSM_P

    def conv_silu(buf_ref, cur_ref, w_ref, bias_ref):
        @pl.when(c == 0)
        def _():
            buf_ref[0:pad, :] = jnp.zeros((pad, buf_ref.shape[1]), F32)

        @pl.when(c > 0)
        def _():
            buf_ref[0:pad, :] = buf_ref[L:L + pad, :]

        buf_ref[pad:pad + L, :] = cur_ref[...]
        acc = bias_ref[...]
        for k in range(SSM_CONV):
            acc = acc + w_ref[k:k + 1, :] * buf_ref[pad - tail + k:pad - tail + k + L, :]
        return _silu(acc)

    xs = conv_silu(bx_ref, xs_ref, cwx_ref, cbx_ref)
    bm = conv_silu(bb_ref, bm_ref, cwb_ref, cbb_ref)
    cm = conv_silu(bc_ref, cm_ref, cwc_ref, cbc_ref)

    @pl.when(c == 0)
    def _():
        st_ref[...] = jnp.zeros(st_ref.shape, F32)

    lane = lax.broadcasted_iota(I32, (1, MISC_W), 1)
    head_lane = (lane >= DT_OFF) & (lane < DT_OFF + SSM_HEADS)
    xdt_in = misc_ref[...] + dtb_ref[...]
    dt = jnp.where(head_lane, jnp.maximum(xdt_in, 0.0) + jnp.log1p(jnp.exp(-jnp.abs(xdt_in))), 0.0)
    adt = dt * aneg_ref[...]
    acum = _dot_exact_l(tri_ref[...], adt)
    acum_t = acum.T
    acum_x = _dot_exact_r(acum, e_ref[...])
    dt_x = _dot_exact_r(dt, e_ref[...])
    expa_x = jnp.exp(acum_x)
    dstate_x = jnp.exp(acum_x[L - 1:L, :] - acum_x)
    xdt = xs * dt_x
    xdt_b = xdt.astype(BF16)
    xw_b = (xdt * dstate_x).astype(BF16)

    li = lax.broadcasted_iota(I32, (L, L), 0)
    si = lax.broadcasted_iota(I32, (L, L), 1)
    lower = li >= si

    for g in range(SSM_G):
        gs = slice(g * gw, (g + 1) * gw)
        ns = slice(g * SSM_N, (g + 1) * SSM_N)
        bm_g = bm[:, ns]
        cm_g = cm[:, ns].astype(BF16)
        cb = _dot_nt(cm_g, bm_g.astype(BF16))
        ys = []
        for e in range(SSM_E):
            h = g * SSM_E + e
            col = acum[:, DT_OFF + h:DT_OFF + h + 1]
            row = acum_t[DT_OFF + h:DT_OFF + h + 1, :]
            lmat = jnp.exp(jnp.where(lower, col - row, NEG_INF))
            ys.append(_dot((cb * lmat).astype(BF16), xdt_b[:, h * SSM_P:(h + 1) * SSM_P]))
        y_diag = jnp.concatenate(ys, axis=-1)
        prev = st_ref[g]
        y_off = _dot(cm_g, prev.astype(BF16)) * expa_x[:, gs]
        st_new = _dot(bm_g.T.astype(BF16), xw_b[:, gs])
        st_ref[g] = expa_x[L - 1:L, gs] * prev + st_new
        y = y_diag + y_off + dx_ref[:, gs] * xs[:, gs]
        y = y * _silu(z_ref[:, gs])
        ms = jnp.mean(y * y, axis=-1, keepdims=True)
        o_ref[:, gs] = ((y * lax.rsqrt(ms + EPS)) * nw_ref[:, gs]).astype(o_ref.dtype)


def _ssd(u, cwx, cwb, cwc, cbx, cbb, cbc, dtb, aneg, dx, nw, tri, expand, b, s):
    nc = s // SSM_L
    L = SSM_L
    row = lambda width, col: pl.BlockSpec((L, width), lambda bi, c: (bi * nc + c, col))
    full = lambda a: pl.BlockSpec(a.shape, lambda bi, c: tuple(0 for _ in a.shape))
    consts = (cwx, cwb, cwc, cbx, cbb, cbc, dtb, aneg, dx, nw, tri, expand)
    return pl.pallas_call(
        _ssd_kernel,
        grid=(b, nc),
        in_specs=[
            row(SSM_D_INNER, COL_Z // SSM_D_INNER),
            row(SSM_D_INNER, COL_XS // SSM_D_INNER),
            row(512, COL_BM // 512),
            row(512, COL_CM // 512),
            row(MISC_W, COL_MISC // MISC_W),
        ] + [full(a) for a in consts],
        out_specs=pl.BlockSpec((L, SSM_D_INNER), lambda bi, c: (bi * nc + c, 0)),
        out_shape=jax.ShapeDtypeStruct((b * s, SSM_D_INNER), BF16),
        scratch_shapes=[
            pltpu.VMEM((L + 8, SSM_D_INNER), F32),
            pltpu.VMEM((L + 8, 512), F32),
            pltpu.VMEM((L + 8, 512), F32),
            pltpu.VMEM((SSM_G, SSM_N, SSM_E * SSM_P), F32),
        ],
        compiler_params=_cparams(("arbitrary", "arbitrary")),
        name="ssd",
    )(u, u, u, u, u, *consts)


def _merge_kernel(x_ref, ya_ref, yb_ref, ga_ref, gb_ref, wa_ref, wb_ref, wo_ref, o_ref):
    pa = _dot(ya_ref[...], wa_ref[...])
    pb = _dot(yb_ref[...], wb_ref[...])
    merged = _sigmoid(ga_ref[...]) * pa + _sigmoid(gb_ref[...]) * pb
    o_ref[...] = x_ref[...] + _dot(merged.astype(BF16), wo_ref[...])


def _merge(x2, ya, yb, u, wa, wb, wo, tm):
    t, d = x2.shape
    full = lambda a: pl.BlockSpec(a.shape, lambda i: (0, 0))
    return pl.pallas_call(
        _merge_kernel,
        grid=(t // tm,),
        in_specs=[
            pl.BlockSpec((tm, d), lambda i: (i, 0)),
            pl.BlockSpec((tm, ya.shape[1]), lambda i: (i, 0)),
            pl.BlockSpec((tm, yb.shape[1]), lambda i: (i, 0)),
            pl.BlockSpec((tm, d), lambda i: (i, COL_GA // D_MODEL)),
            pl.BlockSpec((tm, d), lambda i: (i, COL_GB // D_MODEL)),
            full(wa), full(wb), full(wo),
        ],
        out_specs=pl.BlockSpec((tm, d), lambda i: (i, 0)),
        out_shape=jax.ShapeDtypeStruct((t, d), F32),
        compiler_params=_cparams(("arbitrary",)),
        name="merge",
    )(x2, ya, yb, u, u, wa, wb, wo)


def _ffn_kernel(x_ref, nw_ref, wup_ref, cw_ref, cb_ref, wd_ref, o_ref, buf_ref, carry_ref,
                *, tm, tf, tiles_per_seq):
    i = pl.program_id(0)
    nf = FFN_HIDDEN // tf
    pad = 8
    tail = FFN_CONV - 1
    seq_start = (i % tiles_per_seq) == 0

    @pl.when(i == 0)
    def _():
        carry_ref[...] = jnp.zeros(carry_ref.shape, F32)

    x = x_ref[...]
    ms = jnp.mean(x * x, axis=-1, keepdims=True)
    h = ((x * lax.rsqrt(ms + EPS)) * nw_ref[...]).astype(BF16)

    def up_conv(slot, idx, col0):
        u = _dot(h, wup_ref[:, col0:col0 + tf])
        buf_ref[slot, 0:pad, :] = jnp.where(seq_start, 0.0, carry_ref[idx])
        buf_ref[slot, pad:pad + tm, :] = u
        carry_ref[idx] = u[tm - pad:tm, :]
        acc = cb_ref[:, col0:col0 + tf]
        for k in range(FFN_CONV):
            acc = acc + (cw_ref[k:k + 1, col0:col0 + tf]
                         * buf_ref[slot, pad - tail + k:pad - tail + k + tm, :])
        return acc

    out = x
    for f in range(nf):
        gate = up_conv((2 * f) % 4, 2 * f, f * tf)
        val = up_conv((2 * f + 1) % 4, 2 * f + 1, FFN_HIDDEN + f * tf)
        out = out + _dot((_silu(gate) * val).astype(BF16), wd_ref[f * tf:(f + 1) * tf, :])
    o_ref[...] = out


def _ffn(x1, nw, w_up, conv_w, conv_b, w_down, s, tm, tf):
    t, d = x1.shape
    nf = FFN_HIDDEN // tf
    resident = lambda a: pl.BlockSpec(a.shape, lambda i: (0, 0), pipeline_mode=pl.Buffered(1))
    return pl.pallas_call(
        functools.partial(_ffn_kernel, tm=tm, tf=tf, tiles_per_seq=s // tm),
        grid=(t // tm,),
        in_specs=[
            pl.BlockSpec((tm, d), lambda i: (i, 0)),
            resident(nw), resident(w_up), resident(conv_w), resident(conv_b), resident(w_down),
        ],
        out_specs=pl.BlockSpec((tm, d), lambda i: (i, 0)),
        out_shape=jax.ShapeDtypeStruct((t, d), F32),
        scratch_shapes=[
            pltpu.VMEM((4, tm + 8, tf), F32),
            pltpu.VMEM((2 * nf, 8, tf), F32),
        ],
        compiler_params=_cparams(("arbitrary",)),
        name="ffn",
    )(x1, nw, w_up, conv_w, conv_b, w_down)


def _regroup_w_in(w):
    o = 0
    parts = {}
    for name, width in (("q", 1024), ("kv", 1536), ("gn", 48), ("z", 2048), ("xs", 2048),
                        ("bm", 512), ("cm", 512), ("dt", 32), ("ga", 1024), ("gb", 1024)):
        parts[name] = w[:, o:o + width]
        o += width
    pad = jnp.zeros((w.shape[0], MISC_W - 48 - 32), w.dtype)
    return jnp.concatenate([parts["z"], parts["xs"], parts["q"], parts["ga"], parts["gb"],
                            parts["bm"], parts["cm"], parts["kv"], parts["gn"], parts["dt"], pad],
                           axis=1).astype(BF16)


def _layer(x, norm1_w, w_in, q_gain, k_gain, pe_k, pe_v, wk1, wk2, wv1, wv2, conv_w, conv_b,
           dt_bias, a_log, d_skip, ssm_norm_w, w_proj_a, w_proj_b, w_out, norm2_w, w_up,
           ffn_conv_w, ffn_conv_b, w_down):
    b, s, d = x.shape
    t = b * s
    x2 = x.reshape(t, d)
    tm = min(1024, s)

    u = _in_proj(x2, norm1_w.reshape(1, d), _regroup_w_in(w_in), tm, IN_W_PAD // 7)

    ks, vs, kw, vw, gs = _nsa_prep(u, k_gain, b, s, min(512, s))
    half = (CMP_BLOCK // 2)
    kc, vc = _nsa_cmp(u, pe_k.reshape(2, half * DH), pe_v.reshape(2, half * DH),
                      wk1.astype(BF16), wk2.astype(BF16), wv1.astype(BF16), wv2.astype(BF16),
                      k_gain, b, s)
    tq, ck = 128, 512
    slopes = jnp.exp2(-8.0 * jnp.arange(1, NSA_HEADS + 1, dtype=F32) / NSA_HEADS) * LOG2E
    slope_pieces = jnp.stack([p.astype(F32) for p in _split3(slopes)], axis=1).reshape(-1)
    ya = _nsa_attn(slope_pieces, u, gs, kc, vc, ks, vs, kw, vw, q_gain.reshape(1, DH),
                   b, s, tq, ck)

    lanes = jnp.arange(MISC_W)
    in_heads = (lanes >= DT_OFF) & (lanes < DT_OFF + SSM_HEADS)
    hidx = jnp.clip(lanes - DT_OFF, 0, SSM_HEADS - 1)
    dtb = jnp.where(in_heads, dt_bias[hidx], 0.0).reshape(1, MISC_W)
    aneg = jnp.where(in_heads, -jnp.exp(a_log.astype(F32))[hidx], 0.0).reshape(1, MISC_W)
    e_heads = ((lanes[:, None] - DT_OFF) == (jnp.arange(SSM_D_INNER)[None, :] // SSM_P)).astype(BF16)
    tri = (jnp.arange(SSM_L)[:, None] >= jnp.arange(SSM_L)[None, :]).astype(BF16)
    dx = jnp.repeat(d_skip.astype(F32), SSM_P).reshape(1, SSM_D_INNER)
    yb = _ssd(u, conv_w[:, :2048], conv_w[:, 2048:2560], conv_w[:, 2560:],
              conv_b[:2048].reshape(1, -1), conv_b[2048:2560].reshape(1, -1),
              conv_b[2560:].reshape(1, -1), dtb, aneg, dx, ssm_norm_w.reshape(1, -1), tri,
              e_heads, b, s)

    x1 = _merge(x2, ya, yb, u, w_proj_a.astype(BF16), w_proj_b.astype(BF16), w_out.astype(BF16),
                min(512, s))
    out = _ffn(x1, norm2_w.reshape(1, d), w_up.astype(BF16), ffn_conv_w, ffn_conv_b.reshape(1, -1),
               w_down.astype(BF16), s, min(512, s), 256)
    return out.reshape(b, s, d)


def kernel(x, norm1_w, w_in, nsa_q_gain, nsa_k_gain, cmp_pe_k, cmp_pe_v, cmp_wk1, cmp_wk2, cmp_wv1,
           cmp_wv2, ssm_conv_w, ssm_conv_b, ssm_dt_bias, ssm_a_log, ssm_d, ssm_norm_w, w_proj_a,
           w_proj_b, w_out, norm2_w, ffn_w_up, ffn_conv_w, ffn_conv_b, ffn_w_down):
    for layer in range(norm1_w.shape[0]):
        x = _layer(x, norm1_w[layer], w_in[layer], nsa_q_gain[layer], nsa_k_gain[layer],
                   cmp_pe_k[layer], cmp_pe_v[layer], cmp_wk1[layer], cmp_wk2[layer],
                   cmp_wv1[layer], cmp_wv2[layer], ssm_conv_w[layer], ssm_conv_b[layer],
                   ssm_dt_bias[layer], ssm_a_log[layer], ssm_d[layer], ssm_norm_w[layer],
                   w_proj_a[layer], w_proj_b[layer], w_out[layer], norm2_w[layer],
                   ffn_w_up[layer], ffn_conv_w[layer], ffn_conv_b[layer], ffn_w_down[layer])
    return x
```

```python
import functools

import jax
import jax.numpy as jnp
from jax import lax
from jax.experimental import pallas as pl
from jax.experimental.pallas import tpu as pltpu

F32 = jnp.float32
BF16 = jnp.bfloat16
I32 = jnp.int32

D_MODEL = 1024
NSA_HEADS = 16
NSA_G = 4
NSA_R = 4
DH = 64
CMP_BLOCK = 32
CMP_STRIDE = 16
CMP_HIDDEN = 256
SEL_BLOCK = 64
SEL_TOPK = 8
WINDOW = 512
SSM_D_INNER = 2048
SSM_P = 64
SSM_HEADS = 32
SSM_G = 4
SSM_E = SSM_HEADS // SSM_G
SSM_N = 128
SSM_CONV = 4
SSM_L = 128
FFN_HIDDEN = 2816
FFN_CONV = 3
EPS = 1e-6
NEG_INF = -1e30

COL_Z = 0
COL_XS = 2048
COL_Q = 4096
COL_GA = 5120
COL_GB = 6144
COL_BM = 7168
COL_CM = 7680
COL_KV = 8192
COL_MISC = 9728
IN_W_PAD = 9856
MISC_W = 128
DT_OFF = 3 * NSA_HEADS

AUG_W = 64
AUG_POS = 32
LOG2E = 1.4426950408889634
MASK_BIG = 1e30

VMEM_LIMIT = 56 * 1024 * 1024


def _cparams(sem):
    return pltpu.CompilerParams(dimension_semantics=sem, vmem_limit_bytes=VMEM_LIMIT)


def _sigmoid(x):
    return 1.0 / (1.0 + jnp.exp(-x))


def _silu(x):
    return x * _sigmoid(x)


def _split3(a):
    hi = a.astype(BF16)
    r1 = a - hi.astype(F32)
    mid = r1.astype(BF16)
    lo = (r1 - mid.astype(F32)).astype(BF16)
    return hi, mid, lo


def _dot(a, b):
    return jnp.dot(a, b, preferred_element_type=F32)


def _dot_nt(a, b):
    return lax.dot_general(a, b, (((1,), (1,)), ((), ())), preferred_element_type=F32)


def _dot_exact_r(a, m):
    hi, mid, lo = _split3(a)
    return _dot(hi, m) + _dot(mid, m) + _dot(lo, m)


def _dot_exact_l(m, a):
    hi, mid, lo = _split3(a)
    return _dot(m, hi) + _dot(m, mid) + _dot(m, lo)


def _inproj_kernel(x_ref, nw_ref, w_ref, o_ref, h_ref):
    @pl.when(pl.program_id(1) == 0)
    def _():
        x = x_ref[...]
        ms = jnp.mean(x * x, axis=-1, keepdims=True)
        h_ref[...] = ((x * lax.rsqrt(ms + EPS)) * nw_ref[...]).astype(BF16)

    o_ref[...] = _dot(h_ref[...], w_ref[...])


def _in_proj(x2, nw, w, tm, tn):
    t, d = x2.shape
    n = w.shape[1]
    return pl.pallas_call(
        _inproj_kernel,
        grid=(t // tm, n // tn),
        in_specs=[
            pl.BlockSpec((tm, d), lambda i, j: (i, 0)),
            pl.BlockSpec((1, d), lambda i, j: (0, 0)),
            pl.BlockSpec((d, tn), lambda i, j: (0, j)),
        ],
        out_specs=pl.BlockSpec((tm, tn), lambda i, j: (i, j)),
        out_shape=jax.ShapeDtypeStruct((t, n), F32),
        scratch_shapes=[pltpu.VMEM((tm, d), BF16)],
        compiler_params=_cparams(("arbitrary", "arbitrary")),
        name="in_proj",
    )(x2, nw, w)


def _pos_aug(pos):
    n = pos.shape[0]
    lane = lax.broadcasted_iota(I32, (n, AUG_W), 1)
    blk = lax.shift_right_logical(pos, 6)
    piece = jnp.where((lane & 1) == 0, blk * SEL_BLOCK, pos & (SEL_BLOCK - 1)).astype(F32)
    in_pos = (lane >= AUG_POS) & (lane < AUG_POS + 6)
    return jnp.where(in_pos, piece, jnp.where(lane == blk, 1.0, 0.0))


def _prep_kernel(sel_ref, win_ref, misc_ref, kg_ref, ks_ref, vs_ref, kw_ref, vw_ref, gs_ref, *, ts):
    def knorm(k, gain):
        ms = jnp.mean(k * k, axis=-1, keepdims=True)
        return (k * lax.rsqrt(ms + EPS)) * gain

    sel = sel_ref[...]
    win = win_ref[...]
    sig = _sigmoid(misc_ref[...])
    pos = pl.program_id(1) * ts + lax.broadcasted_iota(I32, (ts, 1), 0)
    aug = _pos_aug(pos)
    for g in range(NSA_G):
        lo = g * DH
        ks_ref[0, g] = jnp.concatenate(
            [knorm(sel[:, lo:lo + DH], kg_ref[1:2, :]), aug], axis=-1).astype(BF16)
        vs_ref[0, g] = sel[:, 256 + lo:256 + lo + DH].astype(BF16)
        kw_ref[0, g] = jnp.concatenate(
            [knorm(win[:, lo:lo + DH], kg_ref[2:3, :]), aug], axis=-1).astype(BF16)
        vw_ref[0, g] = win[:, 256 + lo:256 + lo + DH].astype(BF16)
        gs_ref[0, g] = sig[:, g * 12:(g + 1) * 12]


def _nsa_prep(u, k_gain, b, s, ts):
    nt = s // ts
    k_spec = pl.BlockSpec((1, NSA_G, ts, DH + AUG_W), lambda bi, i: (bi, 0, i, 0))
    k_shape = jax.ShapeDtypeStruct((b, NSA_G, s, DH + AUG_W), BF16)
    kv_spec = pl.BlockSpec((1, NSA_G, ts, DH), lambda bi, i: (bi, 0, i, 0))
    kv_shape = jax.ShapeDtypeStruct((b, NSA_G, s, DH), BF16)
    return pl.pallas_call(
        functools.partial(_prep_kernel, ts=ts),
        grid=(b, nt),
        in_specs=[
            pl.BlockSpec((ts, 512), lambda bi, i: (bi * nt + i, COL_KV // 512 + 1)),
            pl.BlockSpec((ts, 512), lambda bi, i: (bi * nt + i, COL_KV // 512 + 2)),
            pl.BlockSpec((ts, MISC_W), lambda bi, i: (bi * nt + i, COL_MISC // MISC_W)),
            pl.BlockSpec((3, DH), lambda bi, i: (0, 0)),
        ],
        out_specs=[k_spec, kv_spec, k_spec, kv_spec,
                   pl.BlockSpec((1, NSA_G, ts, 12), lambda bi, i: (bi, 0, i, 0))],
        out_shape=[k_shape, kv_shape, k_shape, kv_shape,
                   jax.ShapeDtypeStruct((b, NSA_G, s, 12), F32)],
        compiler_params=_cparams(("arbitrary", "arbitrary")),
        name="nsa_prep",
    )(u, u, u, k_gain)


def _cmp_kernel(k01_ref, k23_ref, v01_ref, v23_ref, pek_ref, pev_ref, wk1_ref, wk2_ref,
                wv1_ref, wv2_ref, kg_ref, kc_ref, vc_ref, *, nseg):
    half = (CMP_BLOCK // 2) * DH

    def compress(src_ref, col0, pe_ref, w1_ref, w2_ref):
        segs = [src_ref[pl.ds(l, nseg, stride=CMP_STRIDE), :][:, col0:col0 + DH]
                for l in range(CMP_STRIDE)]
        xcat = jnp.concatenate(segs, axis=-1)
        ha = _dot((xcat + pe_ref[0:1, :]).astype(BF16), w1_ref[0:half, :])
        hb = _dot((xcat + pe_ref[1:2, :]).astype(BF16), w1_ref[half:2 * half, :])
        hb_next = jnp.concatenate([hb[1:, :], jnp.zeros((1, CMP_HIDDEN), F32)], axis=0)
        hid = ha + hb_next
        return _dot(_silu(hid).astype(BF16), w2_ref[...])

    cend = lax.broadcasted_iota(I32, (nseg, 1), 0) * CMP_STRIDE + (CMP_BLOCK - 1)
    aug = _pos_aug(cend)
    for g in range(NSA_G):
        k_src = k01_ref if g < 2 else k23_ref
        v_src = v01_ref if g < 2 else v23_ref
        col0 = (g % 2) * DH
        kc = compress(k_src, col0, pek_ref, wk1_ref, wk2_ref)
        ms = jnp.mean(kc * kc, axis=-1, keepdims=True)
        kc = (kc * lax.rsqrt(ms + EPS)) * kg_ref[0:1, :]
        kc_ref[0, g] = jnp.concatenate([kc, aug], axis=-1).astype(BF16)
        vc_ref[0, g] = compress(v_src, col0, pev_ref, wv1_ref, wv2_ref).astype(BF16)


def _nsa_cmp(u, pek2, pev2, wk1, wk2, wv1, wv2, k_gain, b, s):
    nseg = s // CMP_STRIDE
    full = lambda shape: pl.BlockSpec(shape, lambda bi: tuple(0 for _ in shape))
    slab = lambda k: pl.BlockSpec((s, 128), lambda bi: (bi, COL_KV // 128 + k))
    out_spec = lambda w: pl.BlockSpec((1, NSA_G, nseg, w), lambda bi: (bi, 0, 0, 0))
    out_shape = lambda w: jax.ShapeDtypeStruct((b, NSA_G, nseg, w), BF16)
    return pl.pallas_call(
        functools.partial(_cmp_kernel, nseg=nseg),
        grid=(b,),
        in_specs=[
            slab(0), slab(1), slab(2), slab(3),
            full(pek2.shape), full(pev2.shape), full(wk1.shape), full(wk2.shape),
            full(wv1.shape), full(wv2.shape), full(k_gain.shape),
        ],
        out_specs=[out_spec(DH + AUG_W), out_spec(DH)],
        out_shape=[out_shape(DH + AUG_W), out_shape(DH)],
        compiler_params=_cparams(("arbitrary",)),
        name="nsa_cmp",
    )(u, u, u, u, pek2, pev2, wk1, wk2, wv1, wv2, k_gain)


def _nsa_kernel(sl_ref, q_ref, gs_ref, kc_ref, vc_ref, ks_ref, vs_ref, kw_ref, vw_ref,
                qg_ref, ovlt_ref, eye_ref, dtab_ref, wtab_ref, o_ref, *, tq, ck, seq):
    g = pl.program_id(1)
    i = pl.program_id(2)
    q0 = i * tq
    rows = NSA_R * tq
    nseg = seq // CMP_STRIDE
    ncmp = (seq - CMP_BLOCK) // CMP_STRIDE + 1
    nb = seq // SEL_BLOCK
    n_sel = min(SEL_TOPK, nb)
    wk = tq + WINDOW
    nbr = ovlt_ref.shape[0]
    per_ck = ck // tq

    q = q_ref[...]
    lane = lax.broadcasted_iota(I32, (tq, AUG_W), 1)
    qparts, cparts = [], []
    for r in range(NSA_R):
        qh = q[:, r * DH:(r + 1) * DH]
        ms = jnp.mean(qh * qh, axis=-1, keepdims=True)
        qparts.append(((qh * lax.rsqrt(ms + EPS)) * qg_ref[...]) * (DH ** -0.5 * LOG2E))
        h = g * NSA_R + r
        cparts.append(jnp.where(
            lane < AUG_POS, 0.0,
            jnp.where(lane < AUG_POS + 2, sl_ref[3 * h],
                      jnp.where(lane < AUG_POS + 4, sl_ref[3 * h + 1],
                                jnp.where(lane < AUG_POS + 6, sl_ref[3 * h + 2], 0.0)))))
    qs = jnp.concatenate(qparts, axis=0)
    cpart = jnp.concatenate(cparts, axis=0)
    qa_base = jnp.concatenate([qs, cpart], axis=-1).astype(BF16)

    tq_col = q0 + lax.broadcasted_iota(I32, (tq, 1), 0)
    t_col = jnp.concatenate([tq_col] * NSA_R, axis=0)

    s_c = _dot_nt(qa_base, kc_ref[0, 0])
    cidx = lax.broadcasted_iota(I32, (1, nseg), 1)
    vis_c = (t_col >= cidx * CMP_STRIDE + (CMP_BLOCK - 1)) & (cidx < ncmp)
    logit_c = jnp.where(vis_c, s_c, NEG_INF)
    e_c = jnp.exp2(logit_c - jnp.max(logit_c, axis=-1, keepdims=True))
    inv_c = (1.0 / jnp.sum(e_c, axis=-1, keepdims=True)) * (t_col >= CMP_BLOCK - 1).astype(F32)
    p_c = e_c * inv_c
    o_c = _dot(p_c.astype(BF16), vc_ref[0, 0])

    p_sum = (p_c[0:tq] + p_c[tq:2 * tq]) + (p_c[2 * tq:3 * tq] + p_c[3 * tq:4 * tq])
    hi, mid, lo = _split3(p_sum)
    ovlt = ovlt_ref[...]
    imp_t = _dot_nt(ovlt, hi) + _dot_nt(ovlt, mid) + _dot_nt(ovlt, lo)

    cur = lax.shift_right_logical(q0 + lax.broadcasted_iota(I32, (1, tq), 1), 6)
    jb = lax.broadcasted_iota(I32, (nbr, tq), 0)
    forced = (jb == 0) | (jb == cur) | (jb == cur - 1)
    causal_blk = jb <= cur
    score = jnp.where(forced, jnp.inf, jnp.where(causal_blk, imp_t, -jnp.inf))
    rank = jnp.zeros((nbr, tq), I32)
    for jp in range(nb):
        sj = score[jp:jp + 1, :]
        beats = (sj > score) | ((sj == score) & (jb > jp))
        rank = rank + beats.astype(I32)
    mask_t = jnp.where((rank < n_sel) & causal_blk, 0.0, -MASK_BIG)
    mask_t = jnp.concatenate([mask_t, jnp.zeros((AUG_W - nbr, tq), F32)], axis=0).astype(BF16)
    mask_q = _dot_nt(eye_ref[...], mask_t)
    qa_sel = jnp.concatenate(
        [qs, cpart + jnp.concatenate([mask_q] * NSA_R, axis=0)], axis=-1).astype(BF16)

    def sel_scores(c):
        k0 = pl.multiple_of(c * ck, ck)
        return _dot_nt(qa_sel, ks_ref[0, 0, pl.ds(k0, ck), :]), k0

    def online(s, k0, carry):
        m_prev, l_prev, acc = carry
        m_new = jnp.maximum(m_prev, jnp.max(s, axis=-1, keepdims=True))
        alpha = jnp.exp2(m_prev - m_new)
        e = jnp.exp2(s - m_new)
        l_new = alpha * l_prev + jnp.sum(e, axis=-1, keepdims=True)
        acc = alpha * acc + _dot(e.astype(BF16), vs_ref[0, 0, pl.ds(k0, ck), :])
        return m_new, l_new, acc

    def sel_chunk(c, carry):
        s, k0 = sel_scores(c)
        return online(s, k0, carry)

    c_last = i // per_ck
    init = (jnp.full((rows, 1), NEG_INF, F32), jnp.zeros((rows, 1), F32),
            jnp.zeros((rows, DH), F32))
    carry = lax.fori_loop(0, c_last, sel_chunk, init)
    s_last, k_last = sel_scores(c_last)
    s_last = s_last + jnp.concatenate([dtab_ref[lax.rem(i, per_ck)]] * NSA_R, axis=0)
    _, l_s, acc_s = online(s_last, k_last, carry)
    o_s = acc_s * (1.0 / l_s)

    w0 = pl.multiple_of(jnp.maximum(q0 - WINDOW, 0), tq)
    wtab = wtab_ref[jnp.minimum(i, WINDOW // tq)]
    s_w = _dot_nt(qa_base, kw_ref[0, 0, pl.ds(w0, wk), :]) + jnp.concatenate([wtab] * NSA_R, axis=0)
    e_w = jnp.exp2(s_w - jnp.max(s_w, axis=-1, keepdims=True))
    o_w = _dot(e_w.astype(BF16), vw_ref[0, 0, pl.ds(w0, wk), :])
    o_w = o_w * (1.0 / jnp.sum(e_w, axis=-1, keepdims=True))

    gs = gs_ref[0, 0]
    outs = []
    for r in range(NSA_R):
        sl = slice(r * tq, (r + 1) * tq)
        outs.append(gs[:, 3 * r:3 * r + 1] * o_c[sl]
                    + gs[:, 3 * r + 1:3 * r + 2] * o_s[sl]
                    + gs[:, 3 * r + 2:3 * r + 3] * o_w[sl])
    o_ref[...] = jnp.concatenate(outs, axis=-1).astype(o_ref.dtype)


def _nsa_tables(s, tq, ck):
    nseg = s // CMP_STRIDE
    nb = s // SEL_BLOCK
    nbr = AUG_POS
    ncmp = (s - CMP_BLOCK) // CMP_STRIDE + 1
    assert nb <= nbr and ck % tq == 0 and WINDOW % tq == 0
    cstart = jnp.arange(nseg) * CMP_STRIDE
    bstart = jnp.arange(nbr) * SEL_BLOCK
    ovlt = ((cstart[None, :] <= bstart[:, None] + SEL_BLOCK - 1)
            & (cstart[None, :] + CMP_BLOCK - 1 >= bstart[:, None])
            & (jnp.arange(nseg)[None, :] < ncmp)
            & (jnp.arange(nbr)[:, None] < nb)).astype(BF16)
    eye = jnp.eye(tq, dtype=BF16)
    qi = jnp.arange(tq)[None, :, None]
    kj = jnp.arange(ck)[None, None, :]
    v = jnp.arange(ck // tq)[:, None, None]
    dtab = jnp.where(kj > v * tq + qi, -MASK_BIG, 0.0).astype(F32)
    wk = tq + WINDOW
    kj = jnp.arange(wk)[None, None, :]
    v = jnp.arange(WINDOW // tq)[:, None, None]
    early = kj > v * tq + qi
    late = jnp.logical_not((kj > qi[0:1]) & (kj <= qi[0:1] + WINDOW))
    wtab = jnp.where(jnp.concatenate([early, late], axis=0), -MASK_BIG, 0.0).astype(F32)
    return ovlt, eye, dtab, wtab


def _nsa_attn(slope_pieces, u, gs, kc, vc, ks, vs, kw, vw, q_gain, b, s, tq, ck):
    nq = s // tq
    nseg = s // CMP_STRIDE
    ovlt, eye, dtab, wtab = _nsa_tables(s, tq, ck)
    per_bg = lambda rows, w: pl.BlockSpec((1, 1, rows, w), lambda bi, g, i: (bi, g, 0, 0))
    const = lambda a: pl.BlockSpec(a.shape, lambda bi, g, i: tuple(0 for _ in a.shape))
    ka = DH + AUG_W
    return pl.pallas_call(
        functools.partial(_nsa_kernel, tq=tq, ck=ck, seq=s),
        grid=(b, NSA_G, nq),
        in_specs=[
            pl.BlockSpec(memory_space=pltpu.SMEM),
            pl.BlockSpec((tq, NSA_R * DH), lambda bi, g, i: (bi * nq + i, COL_Q // 256 + g)),
            pl.BlockSpec((1, 1, tq, 12), lambda bi, g, i: (bi, g, i, 0)),
            per_bg(nseg, ka), per_bg(nseg, DH), per_bg(s, ka), per_bg(s, DH),
            per_bg(s, ka), per_bg(s, DH),
            pl.BlockSpec((1, DH), lambda bi, g, i: (0, 0)),
            const(ovlt), const(eye), const(dtab), const(wtab),
        ],
        out_specs=pl.BlockSpec((tq, NSA_R * DH), lambda bi, g, i: (bi * nq + i, g)),
        out_shape=jax.ShapeDtypeStruct((b * s, NSA_HEADS * DH), BF16),
        compiler_params=_cparams(("arbitrary", "arbitrary", "arbitrary")),
        name="nsa_attn",
    )(slope_pieces, u, gs, kc, vc, ks, vs, kw, vw, q_gain, ovlt, eye, dtab, wtab)


def _ssd_kernel(z_ref, xs_ref, bm_ref, cm_ref, misc_ref, cwx_ref, cwb_ref, cwc_ref,
                cbx_ref, cbb_ref, cbc_ref, dtb_ref, aneg_ref, dx_ref, nw_ref, tri_ref, e_ref,
                o_ref, bx_ref, bb_ref, bc_ref, st_ref):
    c = pl.program_id(1)
    L = SSM_L
    tail = SSM_CONV - 1
    pad = 8
    gw = SSM_E * SSM_P

    def conv_silu(buf_ref, cur_ref, w_ref, bias_ref):
        @pl.when(c == 0)
        def _():
            buf_ref[0:pad, :] = jnp.zeros((pad, buf_ref.shape[1]), F32)

        @pl.when(c > 0)
        def _():
            buf_ref[0:pad, :] = buf_ref[L:L + pad, :]

        buf_ref[pad:pad + L, :] = cur_ref[...]
        acc = bias_ref[...]
        for k in range(SSM_CONV):
            acc = acc + w_ref[k:k + 1, :] * buf_ref[pad - tail + k:pad - tail + k + L, :]
        return _silu(acc)

    xs = conv_silu(bx_ref, xs_ref, cwx_ref, cbx_ref)
    bm = conv_silu(bb_ref, bm_ref, cwb_ref, cbb_ref)
    cm = conv_silu(bc_ref, cm_ref, cwc_ref, cbc_ref)

    @pl.when(c == 0)
    def _():
        st_ref[...] = jnp.zeros(st_ref.shape, F32)

    lane = lax.broadcasted_iota(I32, (1, MISC_W), 1)
    head_lane = (lane >= DT_OFF) & (lane < DT_OFF + SSM_HEADS)
    xdt_in = misc_ref[...] + dtb_ref[...]
    dt = jnp.where(head_lane, jnp.maximum(xdt_in, 0.0) + jnp.log1p(jnp.exp(-jnp.abs(xdt_in))), 0.0)
    adt = dt * aneg_ref[...]
    acum = _dot_exact_l(tri_ref[...], adt)
    acum_t = acum.T
    acum_x = _dot_exact_r(acum, e_ref[...])
    dt_x = _dot_exact_r(dt, e_ref[...])
    expa_x = jnp.exp(acum_x)
    dstate_x = jnp.exp(acum_x[L - 1:L, :] - acum_x)
    xdt = xs * dt_x
    xdt_b = xdt.astype(BF16)
    xw_b = (xdt * dstate_x).astype(BF16)

    li = lax.broadcasted_iota(I32, (L, L), 0)
    si = lax.broadcasted_iota(I32, (L, L), 1)
    lower = li >= si

    for g in range(SSM_G):
        gs = slice(g * gw, (g + 1) * gw)
        ns = slice(g * SSM_N, (g + 1) * SSM_N)
        bm_g = bm[:, ns]
        cm_g = cm[:, ns].astype(BF16)
        cb = _dot_nt(cm_g, bm_g.astype(BF16))
        ys = []
        for e in range(SSM_E):
            h = g * SSM_E + e
            col = acum[:, DT_OFF + h:DT_OFF + h + 1]
            row = acum_t[DT_OFF + h:DT_OFF + h + 1, :]
            lmat = jnp.exp(jnp.where(lower, col - row, NEG_INF))
            ys.append(_dot((cb * lmat).astype(BF16), xdt_b[:, h * SSM_P:(h + 1) * SSM_P]))
        y_diag = jnp.concatenate(ys, axis=-1)
        prev = st_ref[g]
        y_off = _dot(cm_g, prev.astype(BF16)) * expa_x[:, gs]
        st_new = _dot(bm_g.T.astype(BF16), xw_b[:, gs])
        st_ref[g] = expa_x[L - 1:L, gs] * prev + st_new
        y = y_diag + y_off + dx_ref[:, gs] * xs[:, gs]
        y = y * _silu(z_ref[:, gs])
        ms = jnp.mean(y * y, axis=-1, keepdims=True)
        o_ref[:, gs] = ((y * lax.rsqrt(ms + EPS)) * nw_ref[:, gs]).astype(o_ref.dtype)


def _ssd(u, cwx, cwb, cwc, cbx, cbb, cbc, dtb, aneg, dx, nw, tri, expand, b, s):
    nc = s // SSM_L
    L = SSM_L
    row = lambda width, col: pl.BlockSpec((L, width), lambda bi, c: (bi * nc + c, col))
    full = lambda a: pl.BlockSpec(a.shape, lambda bi, c: tuple(0 for _ in a.shape))
    consts = (cwx, cwb, cwc, cbx, cbb, cbc, dtb, aneg, dx, nw, tri, expand)
    return pl.pallas_call(
        _ssd_kernel,
        grid=(b, nc),
        in_specs=[
            row(SSM_D_INNER, COL_Z // SSM_D_INNER),
            row(SSM_D_INNER, COL_XS // SSM_D_INNER),
            row(512, COL_BM // 512),
            row(512, COL_CM // 512),
            row(MISC_W, COL_MISC // MISC_W),
        ] + [full(a) for a in consts],
        out_specs=pl.BlockSpec((L, SSM_D_INNER), lambda bi, c: (bi * nc + c, 0)),
        out_shape=jax.ShapeDtypeStruct((b * s, SSM_D_INNER), BF16),
        scratch_shapes=[
            pltpu.VMEM((L + 8, SSM_D_INNER), F32),
            pltpu.VMEM((L + 8, 512), F32),
            pltpu.VMEM((L + 8, 512), F32),
            pltpu.VMEM((SSM_G, SSM_N, SSM_E * SSM_P), F32),
        ],
        compiler_params=_cparams(("arbitrary", "arbitrary")),
        name="ssd",
    )(u, u, u, u, u, *consts)


def _merge_kernel(x_ref, ya_ref, yb_ref, ga_ref, gb_ref, wa_ref, wb_ref, wo_ref, o_ref):
    pa = _dot(ya_ref[...], wa_ref[...])
    pb = _dot(yb_ref[...], wb_ref[...])
    merged = _sigmoid(ga_ref[...]) * pa + _sigmoid(gb_ref[...]) * pb
    o_ref[...] = x_ref[...] + _dot(merged.astype(BF16), wo_ref[...])


def _merge(x2, ya, yb, u, wa, wb, wo, tm):
    t, d = x2.shape
    full = lambda a: pl.BlockSpec(a.shape, lambda i: (0, 0))
    return pl.pallas_call(
        _merge_kernel,
        grid=(t // tm,),
        in_specs=[
            pl.BlockSpec((tm, d), lambda i: (i, 0)),
            pl.BlockSpec((tm, ya.shape[1]), lambda i: (i, 0)),
            pl.BlockSpec((tm, yb.shape[1]), lambda i: (i, 0)),
            pl.BlockSpec((tm, d), lambda i: (i, COL_GA // D_MODEL)),
            pl.BlockSpec((tm, d), lambda i: (i, COL_GB // D_MODEL)),
            full(wa), full(wb), full(wo),
        ],
        out_specs=pl.BlockSpec((tm, d), lambda i: (i, 0)),
        out_shape=jax.ShapeDtypeStruct((t, d), F32),
        compiler_params=_cparams(("arbitrary",)),
        name="merge",
    )(x2, ya, yb, u, u, wa, wb, wo)


def _ffn_kernel(x_ref, nw_ref, wup_ref, cw_ref, cb_ref, wd_ref, o_ref, buf_ref, carry_ref,
                *, tm, tf, tiles_per_seq):
    i = pl.program_id(0)
    nf = FFN_HIDDEN // tf
    pad = 8
    tail = FFN_CONV - 1
    seq_start = (i % tiles_per_seq) == 0

    @pl.when(i == 0)
    def _():
        carry_ref[...] = jnp.zeros(carry_ref.shape, F32)

    x = x_ref[...]
    ms = jnp.mean(x * x, axis=-1, keepdims=True)
    h = ((x * lax.rsqrt(ms + EPS)) * nw_ref[...]).astype(BF16)

    def up_conv(slot, idx, col0):
        u = _dot(h, wup_ref[:, col0:col0 + tf])
        buf_ref[slot, 0:pad, :] = jnp.where(seq_start, 0.0, carry_ref[idx])
        buf_ref[slot, pad:pad + tm, :] = u
        carry_ref[idx] = u[tm - pad:tm, :]
        acc = cb_ref[:, col0:col0 + tf]
        for k in range(FFN_CONV):
            acc = acc + (cw_ref[k:k + 1, col0:col0 + tf]
                         * buf_ref[slot, pad - tail + k:pad - tail + k + tm, :])
        return acc

    out = x
    for f in range(nf):
        gate = up_conv((2 * f) % 4, 2 * f, f * tf)
        val = up_conv((2 * f + 1) % 4, 2 * f + 1, FFN_HIDDEN + f * tf)
        out = out + _dot((_silu(gate) * val).astype(BF16), wd_ref[f * tf:(f + 1) * tf, :])
    o_ref[...] = out


def _ffn(x1, nw, w_up, conv_w, conv_b, w_down, s, tm, tf):
    t, d = x1.shape
    nf = FFN_HIDDEN // tf
    resident = lambda a: pl.BlockSpec(a.shape, lambda i: (0, 0), pipeline_mode=pl.Buffered(1))
    return pl.pallas_call(
        functools.partial(_ffn_kernel, tm=tm, tf=tf, tiles_per_seq=s // tm),
        grid=(t // tm,),
        in_specs=[
            pl.BlockSpec((tm, d), lambda i: (i, 0)),
            resident(nw), resident(w_up), resident(conv_w), resident(conv_b), resident(w_down),
        ],
        out_specs=pl.BlockSpec((tm, d), lambda i: (i, 0)),
        out_shape=jax.ShapeDtypeStruct((t, d), F32),
        scratch_shapes=[
            pltpu.VMEM((4, tm + 8, tf), F32),
            pltpu.VMEM((2 * nf, 8, tf), F32),
        ],
        compiler_params=_cparams(("arbitrary",)),
        name="ffn",
    )(x1, nw, w_up, conv_w, conv_b, w_down)


def _regroup_w_in(w):
    o = 0
    parts = {}
    for name, width in (("q", 1024), ("kv", 1536), ("gn", 48), ("z", 2048), ("xs", 2048),
                        ("bm", 512), ("cm", 512), ("dt", 32), ("ga", 1024), ("gb", 1024)):
        parts[name] = w[:, o:o + width]
        o += width
    pad = jnp.zeros((w.shape[0], MISC_W - 48 - 32), w.dtype)
    return jnp.concatenate([parts["z"], parts["xs"], parts["q"], parts["ga"], parts["gb"],
                            parts["bm"], parts["cm"], parts["kv"], parts["gn"], parts["dt"], pad],
                           axis=1).astype(BF16)


def _layer(x, norm1_w, w_in, q_gain, k_gain, pe_k, pe_v, wk1, wk2, wv1, wv2, conv_w, conv_b,
           dt_bias, a_log, d_skip, ssm_norm_w, w_proj_a, w_proj_b, w_out, norm2_w, w_up,
           ffn_conv_w, ffn_conv_b, w_down):
    b, s, d = x.shape
    t = b * s
    x2 = x.reshape(t, d)
    tm = min(1024, s)

    u = _in_proj(x2, norm1_w.reshape(1, d), _regroup_w_in(w_in), tm, IN_W_PAD // 7)

    ks, vs, kw, vw, gs = _nsa_prep(u, k_gain, b, s, min(512, s))
    half = (CMP_BLOCK // 2)
    kc, vc = _nsa_cmp(u, pe_k.reshape(2, half * DH), pe_v.reshape(2, half * DH),
                      wk1.astype(BF16), wk2.astype(BF16), wv1.astype(BF16), wv2.astype(BF16),
                      k_gain, b, s)
    tq, ck = 256, 512
    slopes =jnp.exp2(-8.0 * jnp.arange(1, NSA_HEADS + 1, dtype=F32) / NSA_HEADS) * LOG2E
    slope_pieces = jnp.stack([p.astype(F32) for p in _split3(slopes)], axis=1).reshape(-1)
    ya = _nsa_attn(slope_pieces, u, gs, kc, vc, ks, vs, kw, vw, q_gain.reshape(1, DH),
                   b, s, tq, ck)

    lanes = jnp.arange(MISC_W)
    in_heads = (lanes >= DT_OFF) & (lanes < DT_OFF + SSM_HEADS)
    hidx = jnp.clip(lanes - DT_OFF, 0, SSM_HEADS - 1)
    dtb = jnp.where(in_heads, dt_bias[hidx], 0.0).reshape(1, MISC_W)
    aneg = jnp.where(in_heads, -jnp.exp(a_log.astype(F32))[hidx], 0.0).reshape(1, MISC_W)
    e_heads = ((lanes[:, None] - DT_OFF) == (jnp.arange(SSM_D_INNER)[None, :] // SSM_P)).astype(BF16)
    tri = (jnp.arange(SSM_L)[:, None] >= jnp.arange(SSM_L)[None, :]).astype(BF16)
    dx = jnp.repeat(d_skip.astype(F32), SSM_P).reshape(1, SSM_D_INNER)
    yb = _ssd(u, conv_w[:, :2048], conv_w[:, 2048:2560], conv_w[:, 2560:],
              conv_b[:2048].reshape(1, -1), conv_b[2048:2560].reshape(1, -1),
              conv_b[2560:].reshape(1, -1), dtb, aneg, dx, ssm_norm_w.reshape(1, -1), tri,
              e_heads, b, s)

    x1 = _merge(x2, ya, yb, u, w_proj_a.astype(BF16), w_proj_b.astype(BF16), w_out.astype(BF16),
                min(512, s))
    out = _ffn(x1, norm2_w.reshape(1, d), w_up.astype(BF16), ffn_conv_w, ffn_conv_b.reshape(1, -1),
               w_down.astype(BF16), s, min(512, s), 256)
    return out.reshape(b, s, d)


def kernel(x, norm1_w, w_in, nsa_q_gain, nsa_k_gain, cmp_pe_k, cmp_pe_v, cmp_wk1, cmp_wk2, cmp_wv1,
           cmp_wv2, ssm_conv_w, ssm_conv_b, ssm_dt_bias, ssm_a_log, ssm_d, ssm_norm_w, w_proj_a,
           w_proj_b, w_out, norm2_w, ffn_w_up, ffn_conv_w, ffn_conv_b, ffn_w_down):
    params = (norm1_w, w_in, nsa_q_gain, nsa_k_gain, cmp_pe_k, cmp_pe_v, cmp_wk1, cmp_wk2, cmp_wv1,
              cmp_wv2, ssm_conv_w, ssm_conv_b, ssm_dt_bias, ssm_a_log, ssm_d, ssm_norm_w, w_proj_a,
              w_proj_b, w_out, norm2_w, ffn_w_up, ffn_conv_w, ffn_conv_b, ffn_w_down)
    depth = norm1_w.shape[0]
    for layer in range(depth):
        per_layer = [p.reshape(p.shape[1:]) if depth == 1 else p[layer] for p in params]
        x = _layer(x, *per_layer)
    return x
```

```python
import functools

import jax
import jax.numpy as jnp
from jax import lax
from jax.experimental import pallas as pl
from jax.experimental.pallas import tpu as pltpu

F32 = jnp.float32
BF16 = jnp.bfloat16
I32 = jnp.int32

D_MODEL = 1024
NSA_HEADS = 16
NSA_G = 4
NSA_R = 4
DH = 64
CMP_BLOCK = 32
CMP_STRIDE = 16
CMP_HIDDEN = 256
SEL_BLOCK = 64
SEL_TOPK = 8
WINDOW = 512
SSM_D_INNER = 2048
SSM_P = 64
SSM_HEADS = 32
SSM_G = 4
SSM_E = SSM_HEADS // SSM_G
SSM_N = 128
SSM_CONV = 4
SSM_L = 128
FFN_HIDDEN = 2816
FFN_CONV = 3
EPS = 1e-6
NEG_INF = -1e30

COL_Z = 0
COL_XS = 2048
COL_Q = 4096
COL_GA = 5120
COL_GB = 6144
COL_BM = 7168
COL_CM = 7680
COL_KV = 8192
COL_MISC = 9728
IN_W_PAD = 9856
MISC_W = 128
DT_OFF = 3 * NSA_HEADS

AUG_W = 64
AUG_POS = 32
LOG2E = 1.4426950408889634
MASK_BIG = 1e30

VMEM_LIMIT = 56 * 1024 * 1024


def _cparams(sem):
    return pltpu.CompilerParams(dimension_semantics=sem, vmem_limit_bytes=VMEM_LIMIT)


def _sigmoid(x):
    return 1.0 / (1.0 + jnp.exp(-x))


def _silu(x):
    return x * _sigmoid(x)


def _split3(a):
    hi = a.astype(BF16)
    r1 = a - hi.astype(F32)
    mid = r1.astype(BF16)
    lo = (r1 - mid.astype(F32)).astype(BF16)
    return hi, mid, lo


def _dot(a, b):
    return jnp.dot(a, b, preferred_element_type=F32)


def _dot_nt(a, b):
    return lax.dot_general(a, b, (((1,), (1,)), ((), ())), preferred_element_type=F32)


def _dot_exact_r(a, m):
    hi, mid, lo = _split3(a)
    return _dot(hi, m) + _dot(mid, m) + _dot(lo, m)


def _dot_exact_l(m, a):
    hi, mid, lo = _split3(a)
    return _dot(m, hi) + _dot(m, mid) + _dot(m, lo)


def _inproj_kernel(x_ref, nw_ref, w_ref, o_ref, h_ref):
    @pl.when(pl.program_id(1) == 0)
    def _():
        x = x_ref[...]
        ms = jnp.mean(x * x, axis=-1, keepdims=True)
        h_ref[...] = ((x * lax.rsqrt(ms + EPS)) * nw_ref[...]).astype(BF16)

    o_ref[...] = _dot(h_ref[...], w_ref[...])


def _in_proj(x2, nw, w, tm, tn):
    t, d = x2.shape
    n = w.shape[1]
    return pl.pallas_call(
        _inproj_kernel,
        grid=(t // tm, n // tn),
        in_specs=[
            pl.BlockSpec((tm, d), lambda i, j: (i, 0)),
            pl.BlockSpec((1, d), lambda i, j: (0, 0)),
            pl.BlockSpec((d, tn), lambda i, j: (0, j)),
        ],
        out_specs=pl.BlockSpec((tm, tn), lambda i, j: (i, j)),
        out_shape=jax.ShapeDtypeStruct((t, n), F32),
        scratch_shapes=[pltpu.VMEM((tm, d), BF16)],
        compiler_params=_cparams(("arbitrary", "arbitrary")),
        name="in_proj",
    )(x2, nw, w)


def _pos_aug(pos):
    n = pos.shape[0]
    lane = lax.broadcasted_iota(I32, (n, AUG_W), 1)
    blk = lax.shift_right_logical(pos, 6)
    piece = jnp.where((lane & 1) == 0, blk * SEL_BLOCK, pos & (SEL_BLOCK - 1)).astype(F32)
    in_pos = (lane >= AUG_POS) & (lane < AUG_POS + 6)
    return jnp.where(in_pos, piece, jnp.where(lane == blk, 1.0, 0.0))


def _prep_kernel(sel_ref, win_ref, misc_ref, kg_ref, ks_ref, vs_ref, kw_ref, vw_ref, gs_ref, *, ts):
    def knorm(k, gain):
        ms = jnp.mean(k * k, axis=-1, keepdims=True)
        return (k * lax.rsqrt(ms + EPS)) * gain

    sel = sel_ref[...]
    win = win_ref[...]
    sig = _sigmoid(misc_ref[...])
    pos = pl.program_id(1) * ts + lax.broadcasted_iota(I32, (ts, 1), 0)
    aug = _pos_aug(pos)
    for g in range(NSA_G):
        lo = g * DH
        ks_ref[0, g] = jnp.concatenate(
            [knorm(sel[:, lo:lo + DH], kg_ref[1:2, :]), aug], axis=-1).astype(BF16)
        vs_ref[0, g] = sel[:, 256 + lo:256 + lo + DH].astype(BF16)
        kw_ref[0, g] = jnp.concatenate(
            [knorm(win[:, lo:lo + DH], kg_ref[2:3, :]), aug], axis=-1).astype(BF16)
        vw_ref[0, g] = win[:, 256 + lo:256 + lo + DH].astype(BF16)
        gs_ref[0, g] = sig[:, g * 12:(g + 1) * 12]


def _nsa_prep(u, k_gain, b, s, ts):
    nt = s // ts
    k_spec = pl.BlockSpec((1, NSA_G, ts, DH + AUG_W), lambda bi, i: (bi, 0, i, 0))
    k_shape = jax.ShapeDtypeStruct((b, NSA_G, s, DH + AUG_W), BF16)
    kv_spec = pl.BlockSpec((1, NSA_G, ts, DH), lambda bi, i: (bi, 0, i, 0))
    kv_shape = jax.ShapeDtypeStruct((b, NSA_G, s, DH), BF16)
    return pl.pallas_call(
        functools.partial(_prep_kernel, ts=ts),
        grid=(b, nt),
        in_specs=[
            pl.BlockSpec((ts, 512), lambda bi, i: (bi * nt + i, COL_KV // 512 + 1)),
            pl.BlockSpec((ts, 512), lambda bi, i: (bi * nt + i, COL_KV // 512 + 2)),
            pl.BlockSpec((ts, MISC_W), lambda bi, i: (bi * nt + i, COL_MISC // MISC_W)),
            pl.BlockSpec((3, DH), lambda bi, i: (0, 0)),
        ],
        out_specs=[k_spec, kv_spec, k_spec, kv_spec,
                   pl.BlockSpec((1, NSA_G, ts, 12), lambda bi, i: (bi, 0, i, 0))],
        out_shape=[k_shape, kv_shape, k_shape, kv_shape,
                   jax.ShapeDtypeStruct((b, NSA_G, s, 12), F32)],
        compiler_params=_cparams(("arbitrary", "arbitrary")),
        name="nsa_prep",
    )(u, u, u, k_gain)


def _cmp_kernel(k01_ref, k23_ref, v01_ref, v23_ref, pek_ref, pev_ref, wk1_ref, wk2_ref,
                wv1_ref, wv2_ref, kg_ref, kc_ref, vc_ref, *, nseg):
    half = (CMP_BLOCK // 2) * DH

    def compress(src_ref, col0, pe_ref, w1_ref, w2_ref):
        segs = [src_ref[pl.ds(l, nseg, stride=CMP_STRIDE), :][:, col0:col0 + DH]
                for l in range(CMP_STRIDE)]
        xcat = jnp.concatenate(segs, axis=-1)
        ha = _dot((xcat + pe_ref[0:1, :]).astype(BF16), w1_ref[0:half, :])
        hb = _dot((xcat + pe_ref[1:2, :]).astype(BF16), w1_ref[half:2 * half, :])
        hb_next = jnp.concatenate([hb[1:, :], jnp.zeros((1, CMP_HIDDEN), F32)], axis=0)
        hid = ha + hb_next
        return _dot(_silu(hid).astype(BF16), w2_ref[...])

    cend = lax.broadcasted_iota(I32, (nseg, 1), 0) * CMP_STRIDE + (CMP_BLOCK - 1)
    aug = _pos_aug(cend)
    for g in range(NSA_G):
        k_src = k01_ref if g < 2 else k23_ref
        v_src = v01_ref if g < 2 else v23_ref
        col0 = (g % 2) * DH
        kc = compress(k_src, col0, pek_ref, wk1_ref, wk2_ref)
        ms = jnp.mean(kc * kc, axis=-1, keepdims=True)
        kc = (kc * lax.rsqrt(ms + EPS)) * kg_ref[0:1, :]
        kc_ref[0, g] = jnp.concatenate([kc, aug], axis=-1).astype(BF16)
        vc_ref[0, g] = compress(v_src, col0, pev_ref, wv1_ref, wv2_ref).astype(BF16)


def _nsa_cmp(u, pek2, pev2, wk1, wk2, wv1, wv2, k_gain, b, s):
    nseg = s // CMP_STRIDE
    full = lambda shape: pl.BlockSpec(shape, lambda bi: tuple(0 for _ in shape))
    slab = lambda k: pl.BlockSpec((s, 128), lambda bi: (bi, COL_KV // 128 + k))
    out_spec = lambda w: pl.BlockSpec((1, NSA_G, nseg, w), lambda bi: (bi, 0, 0, 0))
    out_shape = lambda w: jax.ShapeDtypeStruct((b, NSA_G, nseg, w), BF16)
    return pl.pallas_call(
        functools.partial(_cmp_kernel, nseg=nseg),
        grid=(b,),
        in_specs=[
            slab(0), slab(1), slab(2), slab(3),
            full(pek2.shape), full(pev2.shape), full(wk1.shape), full(wk2.shape),
            full(wv1.shape), full(wv2.shape), full(k_gain.shape),
        ],
        out_specs=[out_spec(DH + AUG_W), out_spec(DH)],
        out_shape=[out_shape(DH + AUG_W), out_shape(DH)],
        compiler_params=_cparams(("arbitrary",)),
        name="nsa_cmp",
    )(u, u, u, u, pek2, pev2, wk1, wk2, wv1, wv2, k_gain)


def _nsa_kernel(sl_ref, q_ref, gs_ref, kc_ref, vc_ref, ks_ref, vs_ref, kw_ref, vw_ref,
                qg_ref, ovlt_ref, eye_ref, dtab_ref, wtab_ref, o_ref, *, tq, ck, seq):
    g = pl.program_id(1)
    i = pl.program_id(2)
    q0 = i * tq
    rows = NSA_R * tq
    nseg = seq // CMP_STRIDE
    ncmp = (seq - CMP_BLOCK) // CMP_STRIDE + 1
    nb = seq // SEL_BLOCK
    n_sel = min(SEL_TOPK, nb)
    wk = tq + WINDOW
    nbr = ovlt_ref.shape[0]
    per_ck = ck // tq

    q = q_ref[...]
    lane = lax.broadcasted_iota(I32, (tq, AUG_W), 1)
    qparts, cparts = [], []
    for r in range(NSA_R):
        qh = q[:, r * DH:(r + 1) * DH]
        ms = jnp.mean(qh * qh, axis=-1, keepdims=True)
        qparts.append(((qh * lax.rsqrt(ms + EPS)) * qg_ref[...]) * (DH ** -0.5 * LOG2E))
        h = g * NSA_R + r
        cparts.append(jnp.where(
            lane < AUG_POS, 0.0,
            jnp.where(lane < AUG_POS + 2, sl_ref[3 * h],
                      jnp.where(lane < AUG_POS + 4, sl_ref[3 * h + 1],
                                jnp.where(lane < AUG_POS + 6, sl_ref[3 * h + 2], 0.0)))))
    qs = jnp.concatenate(qparts, axis=0)
    cpart = jnp.concatenate(cparts, axis=0)
    qa_base = jnp.concatenate([qs, cpart], axis=-1).astype(BF16)

    tq_col = q0 + lax.broadcasted_iota(I32, (tq, 1), 0)
    t_col = jnp.concatenate([tq_col] * NSA_R, axis=0)

    s_c = _dot_nt(qa_base, kc_ref[0, 0])
    cidx = lax.broadcasted_iota(I32, (1, nseg), 1)
    vis_c = (t_col >= cidx * CMP_STRIDE + (CMP_BLOCK - 1)) & (cidx < ncmp)
    logit_c = jnp.where(vis_c, s_c, NEG_INF)
    e_c = jnp.exp2(logit_c - jnp.max(logit_c, axis=-1, keepdims=True))
    inv_c = (1.0 / jnp.sum(e_c, axis=-1, keepdims=True)) * (t_col >= CMP_BLOCK - 1).astype(F32)
    p_c = e_c * inv_c
    o_c = _dot(p_c.astype(BF16), vc_ref[0, 0])

    p_sum = (p_c[0:tq] + p_c[tq:2 * tq]) + (p_c[2 * tq:3 * tq] + p_c[3 * tq:4 * tq])
    hi, mid, lo = _split3(p_sum)
    ovlt = ovlt_ref[...]
    imp_t = _dot_nt(ovlt, hi) + _dot_nt(ovlt, mid) + _dot_nt(ovlt, lo)

    cur = lax.shift_right_logical(q0 + lax.broadcasted_iota(I32, (1, tq), 1), 6)
    jb = lax.broadcasted_iota(I32, (nbr, tq), 0)
    forced = (jb == 0) | (jb == cur) | (jb == cur - 1)
    causal_blk = jb <= cur
    score = jnp.where(forced, jnp.inf, jnp.where(causal_blk, imp_t, -jnp.inf))
    rank = jnp.zeros((nbr, tq), I32)
    for jp in range(nb):
        sj = score[jp:jp + 1, :]
        beats = (sj > score) | ((sj == score) & (jb > jp))
        rank = rank + beats.astype(I32)
    mask_t = jnp.where((rank < n_sel) & causal_blk, 0.0, -MASK_BIG)
    mask_t = jnp.concatenate([mask_t, jnp.zeros((AUG_W - nbr, tq), F32)], axis=0).astype(BF16)
    mask_q = _dot_nt(eye_ref[...], mask_t)
    qa_sel = jnp.concatenate(
        [qs, cpart + jnp.concatenate([mask_q] * NSA_R, axis=0)], axis=-1).astype(BF16)

    def sel_scores(c):
        k0 = pl.multiple_of(c * ck, ck)
        return _dot_nt(qa_sel, ks_ref[0, 0, pl.ds(k0, ck), :]), k0

    def online(s, k0, carry):
        m_prev, l_prev, acc = carry
        m_new = jnp.maximum(m_prev, jnp.max(s, axis=-1, keepdims=True))
        alpha = jnp.exp2(m_prev - m_new)
        e = jnp.exp2(s - m_new)
        l_new = alpha * l_prev + jnp.sum(e, axis=-1, keepdims=True)
        acc = alpha * acc + _dot(e.astype(BF16), vs_ref[0, 0, pl.ds(k0, ck), :])
        return m_new, l_new, acc

    def sel_chunk(c, carry):
        s, k0 = sel_scores(c)
        return online(s, k0, carry)

    c_last = i // per_ck
    init = (jnp.full((rows, 1), NEG_INF, F32), jnp.zeros((rows, 1), F32),
            jnp.zeros((rows, DH), F32))
    carry = lax.fori_loop(0, c_last, sel_chunk, init)
    s_last, k_last = sel_scores(c_last)
    s_last = s_last + jnp.concatenate([dtab_ref[lax.rem(i, per_ck)]] * NSA_R, axis=0)
    _, l_s, acc_s = online(s_last, k_last, carry)
    o_s = acc_s * (1.0 / l_s)

    w0 = pl.multiple_of(jnp.maximum(q0 - WINDOW, 0), tq)
    wtab = wtab_ref[jnp.minimum(i, WINDOW // tq)]
    s_w = _dot_nt(qa_base, kw_ref[0, 0, pl.ds(w0, wk), :]) + jnp.concatenate([wtab] * NSA_R, axis=0)
    e_w = jnp.exp2(s_w - jnp.max(s_w, axis=-1, keepdims=True))
    o_w = _dot(e_w.astype(BF16), vw_ref[0, 0, pl.ds(w0, wk), :])
    o_w = o_w * (1.0 / jnp.sum(e_w, axis=-1, keepdims=True))

    gs = gs_ref[0, 0]
    outs = []
    for r in range(NSA_R):
        sl = slice(r * tq, (r + 1) * tq)
        outs.append(gs[:, 3 * r:3 * r + 1] * o_c[sl]
                    + gs[:, 3 * r + 1:3 * r + 2] * o_s[sl]
                    + gs[:, 3 * r + 2:3 * r + 3] * o_w[sl])
    o_ref[...] = jnp.concatenate(outs, axis=-1).astype(o_ref.dtype)


def _nsa_tables(s, tq, ck):
    nseg = s // CMP_STRIDE
    nb = s // SEL_BLOCK
    nbr = AUG_POS
    ncmp = (s - CMP_BLOCK) // CMP_STRIDE + 1
    assert nb <= nbr and ck % tq == 0 and WINDOW % tq == 0
    cstart = jnp.arange(nseg) * CMP_STRIDE
    bstart = jnp.arange(nbr) * SEL_BLOCK
    ovlt = ((cstart[None, :] <= bstart[:, None] + SEL_BLOCK - 1)
            & (cstart[None, :] + CMP_BLOCK - 1 >= bstart[:, None])
            & (jnp.arange(nseg)[None, :] < ncmp)
            & (jnp.arange(nbr)[:, None] < nb)).astype(BF16)
    eye = jnp.eye(tq, dtype=BF16)
    qi = jnp.arange(tq)[None, :, None]
    kj = jnp.arange(ck)[None, None, :]
    v = jnp.arange(ck // tq)[:, None, None]
    dtab = jnp.where(kj > v * tq + qi, -MASK_BIG, 0.0).astype(F32)
    wk = tq + WINDOW
    kj = jnp.arange(wk)[None, None, :]
    v = jnp.arange(WINDOW // tq)[:, None, None]
    early = kj > v * tq + qi
    late = jnp.logical_not((kj > qi[0:1]) & (kj <= qi[0:1] + WINDOW))
    wtab = jnp.where(jnp.concatenate([early, late], axis=0), -MASK_BIG, 0.0).astype(F32)
    return ovlt, eye, dtab, wtab


def _nsa_attn(slope_pieces, u, gs, kc, vc, ks, vs, kw, vw, q_gain, b, s, tq, ck):
    nq = s // tq
    nseg = s // CMP_STRIDE
    ovlt, eye, dtab, wtab = _nsa_tables(s, tq, ck)
    per_bg = lambda rows, w: pl.BlockSpec((1, 1, rows, w), lambda bi, g, i: (bi, g, 0, 0))
    const = lambda a: pl.BlockSpec(a.shape, lambda bi, g, i: tuple(0 for _ in a.shape))
    ka = DH + AUG_W
    return pl.pallas_call(
        functools.partial(_nsa_kernel, tq=tq, ck=ck, seq=s),
        grid=(b, NSA_G, nq),
        in_specs=[
            pl.BlockSpec(memory_space=pltpu.SMEM),
            pl.BlockSpec((tq, NSA_R * DH), lambda bi, g, i: (bi * nq + i, COL_Q // 256 + g)),
            pl.BlockSpec((1, 1, tq, 12), lambda bi, g, i: (bi, g, i, 0)),
            per_bg(nseg, ka), per_bg(nseg, DH), per_bg(s, ka), per_bg(s, DH),
            per_bg(s, ka), per_bg(s, DH),
            pl.BlockSpec((1, DH), lambda bi, g, i: (0, 0)),
            const(ovlt), const(eye), const(dtab), const(wtab),
        ],
        out_specs=pl.BlockSpec((tq, NSA_R * DH), lambda bi, g, i: (bi * nq + i, g)),
        out_shape=jax.ShapeDtypeStruct((b * s, NSA_HEADS * DH), BF16),
        compiler_params=_cparams(("arbitrary", "arbitrary", "arbitrary")),
        name="nsa_attn",
    )(slope_pieces, u, gs, kc, vc, ks, vs, kw, vw, q_gain, ovlt, eye, dtab, wtab)


def _ssd_kernel(z_ref, xs_ref, bm_ref, cm_ref, misc_ref, cwx_ref, cwb_ref, cwc_ref,
                cbx_ref, cbb_ref, cbc_ref, dtb_ref, aneg_ref, dx_ref, nw_ref, tri_ref, e_ref,
                o_ref, bx_ref, bb_ref, bc_ref, st_ref):
    c = pl.program_id(1)
    L = SSM_L
    tail = SSM_CONV - 1
    pad = 8
    gw = SSM_E * SSM_P

    def conv_silu(buf_ref, cur_ref, w_ref, bias_ref):
        @pl.when(c == 0)
        def _():
            buf_ref[0:pad, :] = jnp.zeros((pad, buf_ref.shape[1]), F32)

        @pl.when(c > 0)
        def _():
            buf_ref[0:pad, :] = buf_ref[L:L + pad, :]

        buf_ref[pad:pad + L, :] = cur_ref[...]
        acc = bias_ref[...]
        for k in range(SSM_CONV):
            acc = acc + w_ref[k:k + 1, :] * buf_ref[pad - tail + k:pad - tail + k + L, :]
        return _silu(acc)

    xs = conv_silu(bx_ref, xs_ref, cwx_ref, cbx_ref)
    bm = conv_silu(bb_ref, bm_ref, cwb_ref, cbb_ref)
    cm = conv_silu(bc_ref, cm_ref, cwc_ref, cbc_ref)

    @pl.when(c == 0)
    def _():
        st_ref[...] = jnp.zeros(st_ref.shape, F32)

    lane = lax.broadcasted_iota(I32, (1, MISC_W), 1)
    head_lane = (lane >= DT_OFF) & (lane < DT_OFF + SSM_HEADS)
    xdt_in = misc_ref[...] + dtb_ref[...]
    dt = jnp.where(head_lane, jnp.maximum(xdt_in, 0.0) + jnp.log1p(jnp.exp(-jnp.abs(xdt_in))), 0.0)
    adt = dt * aneg_ref[...]
    acum = _dot_exact_l(tri_ref[...], adt)
    acum_t = acum.T
    acum_x = _dot_exact_r(acum, e_ref[...])
    dt_x = _dot_exact_r(dt, e_ref[...])
    expa_x = jnp.exp(acum_x)
    dstate_x = jnp.exp(acum_x[L - 1:L, :] - acum_x)
    xdt = xs * dt_x
    xdt_b = xdt.astype(BF16)
    xw_b = (xdt * dstate_x).astype(BF16)

    li = lax.broadcasted_iota(I32, (L, L), 0)
    si = lax.broadcasted_iota(I32, (L, L), 1)
    lower = li >= si

    for g in range(SSM_G):
        gs = slice(g * gw, (g + 1) * gw)
        ns = slice(g * SSM_N, (g + 1) * SSM_N)
        bm_g = bm[:, ns]
        cm_g = cm[:, ns].astype(BF16)
        cb = _dot_nt(cm_g, bm_g.astype(BF16))
        ys = []
        for e in range(SSM_E):
            h = g * SSM_E + e
            col = acum[:, DT_OFF + h:DT_OFF + h + 1]
            row = acum_t[DT_OFF + h:DT_OFF + h + 1, :]
            lmat = jnp.exp(jnp.where(lower, col - row, NEG_INF))
            ys.append(_dot((cb * lmat).astype(BF16), xdt_b[:, h * SSM_P:(h + 1) * SSM_P]))
        y_diag = jnp.concatenate(ys, axis=-1)
        prev = st_ref[g]
        y_off = _dot(cm_g, prev.astype(BF16)) * expa_x[:, gs]
        st_new = _dot(bm_g.T.astype(BF16), xw_b[:, gs])
        st_ref[g] = expa_x[L - 1:L, gs] * prev + st_new
        y = y_diag + y_off + dx_ref[:, gs] * xs[:, gs]
        y = y * _silu(z_ref[:, gs])
        ms = jnp.mean(y * y, axis=-1, keepdims=True)
        o_ref[:, gs] = ((y * lax.rsqrt(ms + EPS)) * nw_ref[:, gs]).astype(o_ref.dtype)


def _ssd(u, cwx, cwb, cwc, cbx, cbb, cbc, dtb, aneg, dx, nw, tri, expand, b, s):
    nc = s // SSM_L
    L = SSM_L
    row = lambda width, col: pl.BlockSpec((L, width), lambda bi, c: (bi * nc + c, col))
    full = lambda a: pl.BlockSpec(a.shape, lambda bi, c: tuple(0 for _ in a.shape))
    consts = (cwx, cwb, cwc, cbx, cbb, cbc, dtb, aneg, dx, nw, tri, expand)
    return pl.pallas_call(
        _ssd_kernel,
        grid=(b, nc),
        in_specs=[
            row(SSM_D_INNER, COL_Z // SSM_D_INNER),
            row(SSM_D_INNER, COL_XS // SSM_D_INNER),
            row(512, COL_BM // 512),
            row(512, COL_CM // 512),
            row(MISC_W, COL_MISC // MISC_W),
        ] + [full(a) for a in consts],
        out_specs=pl.BlockSpec((L, SSM_D_INNER), lambda bi, c: (bi * nc + c, 0)),
        out_shape=jax.ShapeDtypeStruct((b * s, SSM_D_INNER), BF16),
        scratch_shapes=[
            pltpu.VMEM((L + 8, SSM_D_INNER), F32),
            pltpu.VMEM((L + 8, 512), F32),
            pltpu.VMEM((L + 8, 512), F32),
            pltpu.VMEM((SSM_G, SSM_N, SSM_E * SSM_P), F32),
        ],
        compiler_params=_cparams(("arbitrary", "arbitrary")),
        name="ssd",
    )(u, u, u, u, u, *consts)


def _merge_kernel(x_ref, ya_ref, yb_ref, ga_ref, gb_ref, wa_ref, wb_ref, wo_ref, o_ref):
    pa = _dot(ya_ref[...], wa_ref[...])
    pb = _dot(yb_ref[...], wb_ref[...])
    merged = _sigmoid(ga_ref[...]) * pa + _sigmoid(gb_ref[...]) * pb
    o_ref[...] = x_ref[...] + _dot(merged.astype(BF16), wo_ref[...])


def _merge(x2, ya, yb, u, wa, wb, wo, tm):
    t, d = x2.shape
    full = lambda a: pl.BlockSpec(a.shape, lambda i: (0, 0))
    return pl.pallas_call(
        _merge_kernel,
        grid=(t // tm,),
        in_specs=[
            pl.BlockSpec((tm, d), lambda i: (i, 0)),
            pl.BlockSpec((tm, ya.shape[1]), lambda i: (i, 0)),
            pl.BlockSpec((tm, yb.shape[1]), lambda i: (i, 0)),
            pl.BlockSpec((tm, d), lambda i: (i, COL_GA // D_MODEL)),
            pl.BlockSpec((tm, d), lambda i: (i, COL_GB // D_MODEL)),
            full(wa), full(wb), full(wo),
        ],
        out_specs=pl.BlockSpec((tm, d), lambda i: (i, 0)),
        out_shape=jax.ShapeDtypeStruct((t, d), F32),
        compiler_params=_cparams(("arbitrary",)),
        name="merge",
    )(x2, ya, yb, u, u, wa, wb, wo)


def _ffn_kernel(x_ref, nw_ref, wup_ref, cw_ref, cb_ref, wd_ref, o_ref, buf_ref, carry_ref,
                *, tm, tf, tiles_per_seq):
    i = pl.program_id(0)
    nf = FFN_HIDDEN // tf
    pad = 8
    tail = FFN_CONV - 1
    seq_start = (i % tiles_per_seq) == 0

    @pl.when(i == 0)
    def _():
        carry_ref[...] = jnp.zeros(carry_ref.shape, F32)

    x = x_ref[...]
    ms = jnp.mean(x * x, axis=-1, keepdims=True)
    h = ((x * lax.rsqrt(ms + EPS)) * nw_ref[...]).astype(BF16)

    def up_conv(slot, idx, col0):
        u = _dot(h, wup_ref[:, col0:col0 + tf])
        buf_ref[slot, 0:pad, :] = jnp.where(seq_start, 0.0, carry_ref[idx])
        buf_ref[slot, pad:pad + tm, :] = u
        carry_ref[idx] = u[tm - pad:tm, :]
        acc = cb_ref[:, col0:col0 + tf]
        for k in range(FFN_CONV):
            acc = acc + (cw_ref[k:k + 1, col0:col0 + tf]
                         * buf_ref[slot, pad - tail + k:pad - tail + k + tm, :])
        return acc

    out = x
    for f in range(nf):
        gate = up_conv((2 * f) % 4, 2 * f, f * tf)
        val = up_conv((2 * f + 1) % 4, 2 * f + 1, FFN_HIDDEN + f * tf)
        out = out + _dot((_silu(gate) * val).astype(BF16), wd_ref[f * tf:(f + 1) * tf, :])
    o_ref[...] = out


def _ffn(x1, nw, w_up, conv_w, conv_b, w_down, s, tm, tf):
    t, d = x1.shape
    nf = FFN_HIDDEN // tf
    resident = lambda a: pl.BlockSpec(a.shape, lambda i: (0, 0), pipeline_mode=pl.Buffered(1))
    return pl.pallas_call(
        functools.partial(_ffn_kernel, tm=tm, tf=tf, tiles_per_seq=s // tm),
        grid=(t // tm,),
        in_specs=[
            pl.BlockSpec((tm, d), lambda i: (i, 0)),
            resident(nw), resident(w_up), resident(conv_w), resident(conv_b), resident(w_down),
        ],
        out_specs=pl.BlockSpec((tm, d), lambda i: (i, 0)),
        out_shape=jax.ShapeDtypeStruct((t, d), F32),
        scratch_shapes=[
            pltpu.VMEM((4, tm + 8, tf), F32),
            pltpu.VMEM((2 * nf, 8, tf), F32),
        ],
        compiler_params=_cparams(("arbitrary",)),
        name="ffn",
    )(x1, nw, w_up, conv_w, conv_b, w_down)


def _regroup_w_in(w):
    o = 0
    parts = {}
    for name, width in (("q", 1024), ("kv", 1536), ("gn", 48), ("z", 2048), ("xs", 2048),
                        ("bm", 512), ("cm", 512), ("dt", 32), ("ga", 1024), ("gb", 1024)):
        parts[name] = w[:, o:o + width]
        o += width
    pad = jnp.zeros((w.shape[0], MISC_W - 48 - 32), w.dtype)
    return jnp.concatenate([parts["z"], parts["xs"], parts["q"], parts["ga"], parts["gb"],
                            parts["bm"], parts["cm"], parts["kv"], parts["gn"], parts["dt"], pad],
                           axis=1).astype(BF16)


def _layer(x, norm1_w, w_in, q_gain, k_gain, pe_k, pe_v, wk1, wk2, wv1, wv2, conv_w, conv_b,
           dt_bias, a_log, d_skip, ssm_norm_w, w_proj_a, w_proj_b, w_out, norm2_w, w_up,
           ffn_conv_w, ffn_conv_b, w_down):
    b, s, d = x.shape
    t = b * s
    x2 = x.reshape(t, d)
    tm = min(2048, s)

    u = _in_proj(x2, norm1_w.reshape(1, d), _regroup_w_in(w_in), tm, IN_W_PAD // 11)

    ks, vs, kw, vw, gs = _nsa_prep(u, k_gain, b, s, min(512, s))
    half = (CMP_BLOCK // 2)
    kc, vc = _nsa_cmp(u, pe_k.reshape(2, half * DH), pe_v.reshape(2, half * DH),
                      wk1.astype(BF16), wk2.astype(BF16), wv1.astype(BF16), wv2.astype(BF16),
                      k_gain, b, s)
    tq, ck = 256, 512
    slopes =jnp.exp2(-8.0 * jnp.arange(1, NSA_HEADS + 1, dtype=F32) / NSA_HEADS) * LOG2E
    slope_pieces = jnp.stack([p.astype(F32) for p in _split3(slopes)], axis=1).reshape(-1)
    ya = _nsa_attn(slope_pieces, u, gs, kc, vc, ks, vs, kw, vw, q_gain.reshape(1, DH),
                   b, s, tq, ck)

    lanes = jnp.arange(MISC_W)
    in_heads = (lanes >= DT_OFF) & (lanes < DT_OFF + SSM_HEADS)
    hidx = jnp.clip(lanes - DT_OFF, 0, SSM_HEADS - 1)
    dtb = jnp.where(in_heads, dt_bias[hidx], 0.0).reshape(1, MISC_W)
    aneg = jnp.where(in_heads, -jnp.exp(a_log.astype(F32))[hidx], 0.0).reshape(1, MISC_W)
    e_heads = ((lanes[:, None] - DT_OFF) == (jnp.arange(SSM_D_INNER)[None, :] // SSM_P)).astype(BF16)
    tri = (jnp.arange(SSM_L)[:, None] >= jnp.arange(SSM_L)[None, :]).astype(BF16)
    dx = jnp.repeat(d_skip.astype(F32), SSM_P).reshape(1, SSM_D_INNER)
    yb = _ssd(u, conv_w[:, :2048], conv_w[:, 2048:2560], conv_w[:, 2560:],
              conv_b[:2048].reshape(1, -1), conv_b[2048:2560].reshape(1, -1),
              conv_b[2560:].reshape(1, -1), dtb, aneg, dx, ssm_norm_w.reshape(1, -1), tri,
              e_heads, b, s)

    x1 = _merge(x2, ya, yb, u, w_proj_a.astype(BF16), w_proj_b.astype(BF16), w_out.astype(BF16),
                min(512, s))
    out = _ffn(x1, norm2_w.reshape(1, d), w_up.astype(BF16), ffn_conv_w, ffn_conv_b.reshape(1, -1),
               w_down.astype(BF16), s, min(512, s), 256)
    return out.reshape(b, s, d)


def kernel(x, norm1_w, w_in, nsa_q_gain, nsa_k_gain, cmp_pe_k, cmp_pe_v, cmp_wk1, cmp_wk2, cmp_wv1,
           cmp_wv2, ssm_conv_w, ssm_conv_b, ssm_dt_bias, ssm_a_log, ssm_d, ssm_norm_w, w_proj_a,
           w_proj_b, w_out, norm2_w, ffn_w_up, ffn_conv_w, ffn_conv_b, ffn_w_down):
    params = (norm1_w, w_in, nsa_q_gain, nsa_k_gain, cmp_pe_k, cmp_pe_v, cmp_wk1, cmp_wk2, cmp_wv1,
              cmp_wv2, ssm_conv_w, ssm_conv_b, ssm_dt_bias, ssm_a_log, ssm_d, ssm_norm_w, w_proj_a,
              w_proj_b, w_out, norm2_w, ffn_w_up, ffn_conv_w, ffn_conv_b, ffn_w_down)
    depth = norm1_w.shape[0]
    for layer in range(depth):
        per_layer = [p.reshape(p.shape[1:]) if depth == 1 else p[layer] for p in params]
        x = _layer(x, *per_layer)
    return x
```
